```python
import jax, jax.numpy as jnp
from jax import lax
import numpy as np

D_MODEL = 2048
BATCH = 1
SEQ = 8192
DEPTH = 1

D_FF = 5632
CONV_WIDTH = 4
RG_WIDTH = 2048
RG_BLOCKS = 8
RG_BLOCK = RG_WIDTH // RG_BLOCKS
RG_C = 8.0
M_HEADS = 8
M_HEAD_DIM = 256
M_WIDTH = M_HEADS * M_HEAD_DIM
M_CHUNK = 128
NORM_EPS = 1e-6
IN_SIZES = (RG_WIDTH, RG_WIDTH, M_WIDTH, M_WIDTH, D_MODEL, D_MODEL, M_HEADS, M_HEADS)
D_IN = 2 * RG_WIDTH + 2 * M_WIDTH + 2 * D_MODEL + 2 * M_HEADS

kernel_name = 'hybrid_rglru_mlstm_macaron'


def rms_norm(x, g):
    x32 = x.astype(jnp.float32)
    y = x32 * lax.rsqrt(jnp.mean(x32 * x32, axis=-1, keepdims=True) + NORM_EPS)
    return (y * g.astype(jnp.float32)).astype(x.dtype)


def swiglu_ffn(u, w_gate, w_up, w_down):
    return (jax.nn.silu(u @ w_gate) * (u @ w_up)) @ w_down


def causal_depthwise_conv(x, w, b):
    k, c = w.shape
    y = lax.conv_general_dilated(x, w[:, None, :].astype(x.dtype), window_strides=(1,),
                                 padding=[(k - 1, 0)], dimension_numbers=('NWC', 'WIO', 'NWC'),
                                 feature_group_count=c)
    return y + b.astype(x.dtype)


def _linear_recurrence_combine(left, right):
    a_l, b_l = left
    a_r, b_r = right
    return a_l * a_r, a_r * b_l + b_r


def rglru_branch(u_x, u_gate, conv_w, conv_b, w_a, b_a, w_x, b_x, lam):
    bsz, s, _ = u_x.shape
    xc = causal_depthwise_conv(u_x, conv_w, conv_b).astype(jnp.float32)
    xb = xc.reshape(bsz, s, RG_BLOCKS, RG_BLOCK)
    r = jax.nn.sigmoid(jnp.einsum('bsnd,nde->bsne', xb, w_a.astype(jnp.float32)) + b_a.astype(jnp.float32))
    i = jax.nn.sigmoid(jnp.einsum('bsnd,nde->bsne', xb, w_x.astype(jnp.float32)) + b_x.astype(jnp.float32))
    r = r.reshape(bsz, s, RG_WIDTH)
    i = i.reshape(bsz, s, RG_WIDTH)
    log_a = -RG_C * r * jax.nn.softplus(-lam.astype(jnp.float32))
    a = jnp.exp(log_a)
    beta = jnp.sqrt(-jnp.expm1(2.0 * log_a)) * (i * xc)
    _, h = lax.associative_scan(_linear_recurrence_combine, (a, beta), axis=1)
    return h.astype(u_x.dtype) * jax.nn.gelu(u_gate)


def mlstm_chunkwise(q, k, v, log_i, log_f):
    bsz, nh, s, d = q.shape
    nc, L = s // M_CHUNK, M_CHUNK
    q = q.reshape(bsz, nh, nc, L, d)
    k = k.reshape(bsz, nh, nc, L, d)
    v = v.reshape(bsz, nh, nc, L, d)
    log_i = log_i.reshape(bsz, nh, nc, L)
    b = jnp.cumsum(log_f.reshape(bsz, nh, nc, L), axis=-1)
    g = b[..., -1]
    a = g[..., None] - b + log_i
    m_loc = jnp.max(a, axis=-1)
    w = jnp.exp(a - m_loc[..., None])
    c_loc = jnp.einsum('bhcl,bhclk,bhclv->bhckv', w, k, v)
    n_loc = jnp.einsum('bhcl,bhclk->bhck', w, k)

    def step(carry, inp):
        c_st, n_st, m_st = carry
        g_c, m_c, c_c, n_c = inp
        m_new = jnp.maximum(g_c + m_st, m_c)
        s_old = jnp.exp(g_c + m_st - m_new)
        s_loc = jnp.exp(m_c - m_new)
        c_new = s_old[..., None, None] * c_st + s_loc[..., None, None] * c_c
        n_new = s_old[..., None] * n_st + s_loc[..., None] * n_c
        return (c_new, n_new, m_new), (c_st, n_st, m_st)

    init = (jnp.zeros((bsz, nh, d, d), jnp.float32), jnp.zeros((bsz, nh, d), jnp.float32),
            jnp.zeros((bsz, nh), jnp.float32))
    xs = (jnp.moveaxis(g, 2, 0), jnp.moveaxis(m_loc, 2, 0), jnp.moveaxis(c_loc, 2, 0), jnp.moveaxis(n_loc, 2, 0))
    _, (c_prev, n_prev, m_prev) = lax.scan(step, init, xs)
    c_prev = jnp.moveaxis(c_prev, 0, 2)
    n_prev = jnp.moveaxis(n_prev, 0, 2)
    m_prev = jnp.moveaxis(m_prev, 0, 2)

    causal = jnp.tril(jnp.ones((L, L), dtype=bool))
    log_d = b[..., :, None] - b[..., None, :] + log_i[..., None, :]
    log_d = jnp.where(causal, log_d, -jnp.inf)
    log_inter = b + m_prev[..., None]
    m_t = jnp.maximum(log_inter, jnp.max(log_d, axis=-1))
    scores = jnp.einsum('bhctd,bhcsd->bhcts', q, k) * jnp.exp(log_d - m_t[..., None])
    inter_w = jnp.exp(log_inter - m_t)
    num = (jnp.einsum('bhcts,bhcsv->bhctv', scores, v)
           + inter_w[..., None] * jnp.einsum('bhctk,bhckv->bhctv', q, c_prev))
    den = jnp.sum(scores, axis=-1) + inter_w * jnp.einsum('bhctk,bhck->bhct', q, n_prev)
    h = num / jnp.maximum(jnp.abs(den), jnp.exp(-m_t))[..., None]
    return h.reshape(bsz, nh, s, d)


def head_layer_norm(h, g):
    mu = jnp.mean(h, axis=-1, keepdims=True)
    var = jnp.mean(jnp.square(h - mu), axis=-1, keepdims=True)
    y = (h - mu) * lax.rsqrt(var + NORM_EPS)
    return y.reshape(h.shape[0], h.shape[1], -1) * g.astype(jnp.float32)


def mlstm_branch(u_x, u_o, pre_i, pre_f, conv_w, conv_b, w_q, w_k, w_v, b_i, b_f, norm_g, skip):
    bsz, s, _ = u_x.shape
    xc = jax.nn.silu(causal_depthwise_conv(u_x, conv_w, conv_b))
    xch = xc.reshape(bsz, s, M_HEADS, M_HEAD_DIM)
    uxh = u_x.reshape(bsz, s, M_HEADS, M_HEAD_DIM)
    q = jnp.einsum('bshd,hde->bhse', xch, w_q).astype(jnp.float32)
    k = jnp.einsum('bshd,hde->bhse', xch, w_k).astype(jnp.float32) * (M_HEAD_DIM ** -0.5)
    v = jnp.einsum('bshd,hde->bhse', uxh, w_v).astype(jnp.float32)
    log_i = jnp.swapaxes((pre_i + b_i).astype(jnp.float32), 1, 2)
    log_f = jnp.swapaxes(jax.nn.log_sigmoid((pre_f + b_f).astype(jnp.float32)), 1, 2)
    h = mlstm_chunkwise(q, k, v, log_i, log_f)
    h = head_layer_norm(jnp.swapaxes(h, 1, 2), norm_g)
    h = h + skip.astype(jnp.float32) * xc.astype(jnp.float32)
    return (jax.nn.sigmoid(u_o.astype(jnp.float32)) * h).astype(u_x.dtype)


def hybrid_mixer(u, w_in, rg_conv_w, rg_conv_b, rg_w_a, rg_b_a, rg_w_x, rg_b_x, rg_lambda,
                 m_conv_w, m_conv_b, m_w_q, m_w_k, m_w_v, m_b_i, m_b_f, m_norm, m_skip,
                 w_proj_rg, w_proj_m, w_out):
    proj = u @ w_in
    offsets = []
    acc = 0
    for size in IN_SIZES[:-1]:
        acc += size
        offsets.append(acc)
    rg_x, rg_gate, m_x, m_o, gate_rg, gate_m, m_i, m_f = jnp.split(proj, offsets, axis=-1)
    y_rg = rglru_branch(rg_x, rg_gate, rg_conv_w, rg_conv_b, rg_w_a, rg_b_a, rg_w_x, rg_b_x, rg_lambda)
    y_m = mlstm_branch(m_x, m_o, m_i, m_f, m_conv_w, m_conv_b, m_w_q, m_w_k, m_w_v, m_b_i, m_b_f, m_norm, m_skip)
    merged = jax.nn.sigmoid(gate_rg) * (y_rg @ w_proj_rg) + jax.nn.sigmoid(gate_m) * (y_m @ w_proj_m)
    return merged @ w_out


def setup_inputs(seed: int = 0) -> dict:
    key = jax.random.key(seed)
    ks = iter(jax.random.split(key, 40))
    f32 = jnp.float32

    def nrm(shape, scale):
        return jax.random.normal(next(ks), shape, f32) * scale

    L = DEPTH
    x = nrm((BATCH, SEQ, D_MODEL), 1.0)
    ffn1_norm = 1.0 + nrm((L, D_MODEL), 0.1)
    ffn1_w_gate = nrm((L, D_MODEL, D_FF), D_MODEL ** -0.5)
    ffn1_w_up = nrm((L, D_MODEL, D_FF), D_MODEL ** -0.5)
    ffn1_w_down = nrm((L, D_FF, D_MODEL), D_FF ** -0.5)
    mix_norm = 1.0 + nrm((L, D_MODEL), 0.1)
    w_in = nrm((L, D_MODEL, D_IN), D_MODEL ** -0.5)
    rg_conv_w = nrm((L, CONV_WIDTH, RG_WIDTH), CONV_WIDTH ** -0.5)
    rg_conv_b = nrm((L, RG_WIDTH), 0.01)
    rg_w_a = nrm((L, RG_BLOCKS, RG_BLOCK, RG_BLOCK), RG_BLOCK ** -0.5)
    rg_b_a = nrm((L, RG_BLOCKS, RG_BLOCK), 0.01)
    rg_w_x = nrm((L, RG_BLOCKS, RG_BLOCK, RG_BLOCK), RG_BLOCK ** -0.5)
    rg_b_x = nrm((L, RG_BLOCKS, RG_BLOCK), 0.01)
    a_base = jax.random.uniform(next(ks), (L, RG_WIDTH), f32, minval=0.9, maxval=0.999)
    rg_lambda = jnp.log(a_base) - jnp.log1p(-a_base)
    m_conv_w = nrm((L, CONV_WIDTH, M_WIDTH), CONV_WIDTH ** -0.5)
    m_conv_b = nrm((L, M_WIDTH), 0.01)
    m_w_q = nrm((L, M_HEADS, M_HEAD_DIM, M_HEAD_DIM), M_HEAD_DIM ** -0.5)
    m_w_k = nrm((L, M_HEADS, M_HEAD_DIM, M_HEAD_DIM), M_HEAD_DIM ** -0.5)
    m_w_v = nrm((L, M_HEADS, M_HEAD_DIM, M_HEAD_DIM), M_HEAD_DIM ** -0.5)
    m_b_i = nrm((L, M_HEADS), 0.1)
    m_b_f = jnp.linspace(3.0, 6.0, M_HEADS, dtype=f32)[None, :] + nrm((L, M_HEADS), 0.1)
    m_norm = 1.0 + nrm((L, M_WIDTH), 0.1)
    m_skip = 1.0 + nrm((L, M_WIDTH), 0.1)
    w_proj_rg = nrm((L, RG_WIDTH, D_MODEL), RG_WIDTH ** -0.5)
    w_proj_m = nrm((L, M_WIDTH, D_MODEL), M_WIDTH ** -0.5)
    w_out = nrm((L, D_MODEL, D_MODEL), D_MODEL ** -0.5)
    ffn2_norm = 1.0 + nrm((L, D_MODEL), 0.1)
    ffn2_w_gate = nrm((L, D_MODEL, D_FF), D_MODEL ** -0.5)
    ffn2_w_up = nrm((L, D_MODEL, D_FF), D_MODEL ** -0.5)
    ffn2_w_down = nrm((L, D_FF, D_MODEL), D_FF ** -0.5)
    final_norm = 1.0 + nrm((D_MODEL,), 0.1)
    return {'x': x, 'ffn1_norm': ffn1_norm, 'ffn1_w_gate': ffn1_w_gate, 'ffn1_w_up': ffn1_w_up,
            'ffn1_w_down': ffn1_w_down, 'mix_norm': mix_norm, 'w_in': w_in,
            'rg_conv_w': rg_conv_w, 'rg_conv_b': rg_conv_b, 'rg_w_a': rg_w_a, 'rg_b_a': rg_b_a,
            'rg_w_x': rg_w_x, 'rg_b_x': rg_b_x, 'rg_lambda': rg_lambda,
            'm_conv_w': m_conv_w, 'm_conv_b': m_conv_b, 'm_w_q': m_w_q, 'm_w_k': m_w_k, 'm_w_v': m_w_v,
            'm_b_i': m_b_i, 'm_b_f': m_b_f, 'm_norm': m_norm, 'm_skip': m_skip,
            'w_proj_rg': w_proj_rg, 'w_proj_m': w_proj_m, 'w_out': w_out,
            'ffn2_norm': ffn2_norm, 'ffn2_w_gate': ffn2_w_gate, 'ffn2_w_up': ffn2_w_up,
            'ffn2_w_down': ffn2_w_down, 'final_norm': final_norm}


def reference(x, ffn1_norm, ffn1_w_gate, ffn1_w_up, ffn1_w_down, mix_norm, w_in,
              rg_conv_w, rg_conv_b, rg_w_a, rg_b_a, rg_w_x, rg_b_x, rg_lambda,
              m_conv_w, m_conv_b, m_w_q, m_w_k, m_w_v, m_b_i, m_b_f, m_norm, m_skip,
              w_proj_rg, w_proj_m, w_out, ffn2_norm, ffn2_w_gate, ffn2_w_up, ffn2_w_down, final_norm):
    h = x
    for l in range(DEPTH):
        h = h + 0.5 * swiglu_ffn(rms_norm(h, ffn1_norm[l]), ffn1_w_gate[l], ffn1_w_up[l], ffn1_w_down[l])
        h = h + hybrid_mixer(rms_norm(h, mix_norm[l]), w_in[l],
                             rg_conv_w[l], rg_conv_b[l], rg_w_a[l], rg_b_a[l], rg_w_x[l], rg_b_x[l], rg_lambda[l],
                             m_conv_w[l], m_conv_b[l], m_w_q[l], m_w_k[l], m_w_v[l], m_b_i[l], m_b_f[l],
                             m_norm[l], m_skip[l], w_proj_rg[l], w_proj_m[l], w_out[l])
        h = h + 0.5 * swiglu_ffn(rms_norm(h, ffn2_norm[l]), ffn2_w_gate[l], ffn2_w_up[l], ffn2_w_down[l])
    return rms_norm(h, final_norm)
```

```python
import functools

import jax
import jax.numpy as jnp
from jax import lax
from jax.experimental import pallas as pl
from jax.experimental.pallas import tpu as pltpu

D_MODEL = 2048
SEQ = 8192
D_FF = 5632
CONV_WIDTH = 4
RG_WIDTH = 2048
RG_BLOCKS = 8
RG_BLOCK = RG_WIDTH // RG_BLOCKS
RG_C = 8.0
M_HEADS = 8
M_HEAD_DIM = 256
M_WIDTH = M_HEADS * M_HEAD_DIM
M_CHUNK = 128
NORM_EPS = 1e-6

LANES = 128
SUBLANES = 8
V7X_VMEM_BYTES = 64 * 1024 * 1024
VMEM_LIMIT_BYTES = (V7X_VMEM_BYTES // 8) * 7

COL_RG_X, COL_RG_GATE, COL_M_X, COL_M_O, COL_GATE_RG, COL_GATE_M = range(6)
N_WIDE = 6 * D_MODEL

FFN_TM, FFN_TF = 512, 512
INPROJ_TM, INPROJ_TN = 1024, 1024
RG_TS = 512
MERGE_TM, MERGE_TN = 512, 512

BF16 = jnp.bfloat16
F32 = jnp.float32


def _rms_norm(x, g):
    ms = jnp.mean(x * x, axis=-1, keepdims=True)
    return x * lax.rsqrt(ms + NORM_EPS) * g


def _softplus(z):
    return jnp.maximum(z, 0.0) + jnp.log1p(jnp.exp(-jnp.abs(z)))


def _compiler_params(semantics):
    return pltpu.CompilerParams(dimension_semantics=semantics,
                                vmem_limit_bytes=VMEM_LIMIT_BYTES)


def _ffn_kernel(*refs, final_norm):
    if final_norm:
        x_ref, g_ref, wg_ref, wu_ref, wd_ref, fg_ref, o_ref, u_ref, acc_ref = refs
    else:
        x_ref, g_ref, wg_ref, wu_ref, wd_ref, o_ref, u_ref, acc_ref = refs
    j = pl.program_id(1)

    @pl.when(j == 0)
    def _():
        u_ref[...] = _rms_norm(x_ref[...], g_ref[...]).astype(BF16)
        acc_ref[...] = jnp.zeros_like(acc_ref)

    u = u_ref[...]
    gate = jnp.dot(u, wg_ref[...], preferred_element_type=F32)
    up = jnp.dot(u, wu_ref[...], preferred_element_type=F32)
    act = (gate * jax.nn.sigmoid(gate) * up).astype(BF16)
    acc_ref[...] += jnp.dot(act, wd_ref[...], preferred_element_type=F32)

    @pl.when(j == pl.num_programs(1) - 1)
    def _():
        h = x_ref[...] + 0.5 * acc_ref[...]
        if final_norm:
            h = _rms_norm(h, fg_ref[...])
        o_ref[...] = h


def _ffn(x, norm_g, wg, wu, wd, final_g=None):
    s, d = x.shape
    f = wg.shape[1]
    tm, tf = FFN_TM, FFN_TF
    final_norm = final_g is not None
    in_specs = [
        pl.BlockSpec((tm, d), lambda i, j: (i, 0)),
        pl.BlockSpec((1, d), lambda i, j: (0, 0)),
        pl.BlockSpec((d, tf), lambda i, j: (0, j)),
        pl.BlockSpec((d, tf), lambda i, j: (0, j)),
        pl.BlockSpec((tf, d), lambda i, j: (j, 0)),
    ]
    args = [x, norm_g, wg, wu, wd]
    if final_norm:
        in_specs.append(pl.BlockSpec((1, d), lambda i, j: (0, 0)))
        args.append(final_g)
    return pl.pallas_call(
        functools.partial(_ffn_kernel, final_norm=final_norm),
        grid=(s // tm, f // tf),
        in_specs=in_specs,
        out_specs=pl.BlockSpec((tm, d), lambda i, j: (i, 0)),
        out_shape=jax.ShapeDtypeStruct((s, d), F32),
        scratch_shapes=[pltpu.VMEM((tm, d), BF16), pltpu.VMEM((tm, d), F32)],
        compiler_params=_compiler_params(("parallel", "arbitrary")),
        name="ffn_final" if final_norm else "ffn",
    )(*args)


def _inproj_kernel(h_ref, g_ref, w_ref, wif_ref, o_ref, gif_ref, u_ref):
    j = pl.program_id(1)

    @pl.when(j == 0)
    def _():
        u = _rms_norm(h_ref[...], g_ref[...]).astype(BF16)
        u_ref[...] = u
        gif_ref[...] = jnp.dot(u, wif_ref[...], preferred_element_type=F32)

    o_ref[...] = jnp.dot(u_ref[...], w_ref[...], preferred_element_type=F32)


def _inproj(h, norm_g, w_wide, w_if):
    s, d = h.shape
    n = w_wide.shape[1]
    tm, tn = INPROJ_TM, INPROJ_TN
    return pl.pallas_call(
        _inproj_kernel,
        grid=(s // tm, n // tn),
        in_specs=[
            pl.BlockSpec((tm, d), lambda i, j: (i, 0)),
            pl.BlockSpec((1, d), lambda i, j: (0, 0)),
            pl.BlockSpec((d, tn), lambda i, j: (0, j)),
            pl.BlockSpec((d, LANES), lambda i, j: (0, 0)),
        ],
        out_specs=[
            pl.BlockSpec((tm, tn), lambda i, j: (i, j)),
            pl.BlockSpec((tm, LANES), lambda i, j: (i, 0)),
        ],
        out_shape=[jax.ShapeDtypeStruct((s, n), F32),
                   jax.ShapeDtypeStruct((s, LANES), F32)],
        scratch_shapes=[pltpu.VMEM((tm, d), BF16)],
        compiler_params=_compiler_params(("parallel", "arbitrary")),
        name="in_proj",
    )(h, norm_g, w_wide, w_if)


def _causal_conv(x, xbuf_ref, cw, cb, rows):
    xbuf_ref[pl.ds(SUBLANES, rows), :] = x
    out = cb + cw[CONV_WIDTH - 1:CONV_WIDTH, :] * x
    for back in range(1, CONV_WIDTH):
        tap = CONV_WIDTH - 1 - back
        out = out + cw[tap:tap + 1, :] * xbuf_ref[pl.ds(SUBLANES - back, rows), :]
    xbuf_ref[pl.ds(0, SUBLANES), :] = xbuf_ref[pl.ds(rows, SUBLANES), :]
    return out


def _rglru_kernel(x_ref, gate_ref, cw_ref, cb_ref, wa_ref, ba_ref, wx_ref, bx_ref, lam_ref,
                  o_ref, xbuf_ref, a_ref, b_ref, h_ref, carry_ref):
    ts = x_ref.shape[0]

    @pl.when(pl.program_id(1) == 0)
    def _():
        xbuf_ref[pl.ds(0, SUBLANES), :] = jnp.zeros((SUBLANES, RG_BLOCK), F32)
        carry_ref[...] = jnp.zeros_like(carry_ref)

    xc = _causal_conv(x_ref[...], xbuf_ref, cw_ref[...], cb_ref[...], ts)
    xcb = xc.astype(BF16)
    r = jax.nn.sigmoid(jnp.dot(xcb, wa_ref[0], preferred_element_type=F32) + ba_ref[0])
    i = jax.nn.sigmoid(jnp.dot(xcb, wx_ref[0], preferred_element_type=F32) + bx_ref[0])
    log_a = -RG_C * r * _softplus(-lam_ref[...])
    a = jnp.exp(log_a)
    a_ref[...] = a
    b_ref[...] = jnp.sqrt(-jnp.tanh(log_a) * (a * a + 1.0)) * (i * xc)

    row = lax.broadcasted_iota(jnp.int32, (SUBLANES, RG_BLOCK), 0)

    def group(gi, carry):
        r0 = pl.multiple_of(gi * SUBLANES, SUBLANES)
        a = a_ref[pl.ds(r0, SUBLANES), :]
        b = b_ref[pl.ds(r0, SUBLANES), :]
        for k in (1, 2, 4):
            keep = row >= k
            a_prev = pltpu.roll(a, k, axis=0)
            b_prev = pltpu.roll(b, k, axis=0)
            b = jnp.where(keep, a * b_prev + b, b)
            a = jnp.where(keep, a * a_prev, a)
        h = b + a * carry
        h_ref[pl.ds(r0, SUBLANES), :] = h
        return h[SUBLANES - 1:SUBLANES, :]

    carry_ref[...] = lax.fori_loop(0, ts // SUBLANES, group, carry_ref[...], unroll=4)
    o_ref[...] = (h_ref[...] * jax.nn.gelu(gate_ref[...])).astype(o_ref.dtype)


def _rglru(proj, conv_w, conv_b, w_a, b_a, w_x, b_x, lam):
    s = proj.shape[0]
    ts = RG_TS
    blk = RG_BLOCK
    vec = lambda n, t: (0, n)
    return pl.pallas_call(
        _rglru_kernel,
        grid=(RG_BLOCKS, s // ts),
        in_specs=[
            pl.BlockSpec((ts, blk), lambda n, t: (t, COL_RG_X * RG_BLOCKS + n)),
            pl.BlockSpec((ts, blk), lambda n, t: (t, COL_RG_GATE * RG_BLOCKS + n)),
            pl.BlockSpec((CONV_WIDTH, blk), vec),
            pl.BlockSpec((1, blk), vec),
            pl.BlockSpec((1, blk, blk), lambda n, t: (n, 0, 0)),
            pl.BlockSpec((1, 1, blk), lambda n, t: (n, 0, 0)),
            pl.BlockSpec((1, blk, blk), lambda n, t: (n, 0, 0)),
            pl.BlockSpec((1, 1, blk), lambda n, t: (n, 0, 0)),
            pl.BlockSpec((1, blk), vec),
        ],
        out_specs=pl.BlockSpec((ts, blk), lambda n, t: (t, n)),
        out_shape=jax.ShapeDtypeStruct((s, RG_WIDTH), BF16),
        scratch_shapes=[
            pltpu.VMEM((ts + SUBLANES, blk), F32),
            pltpu.VMEM((ts, blk), F32),
            pltpu.VMEM((ts, blk), F32),
            pltpu.VMEM((ts, blk), F32),
            pltpu.VMEM((1, blk), F32),
        ],
        compiler_params=_compiler_params(("parallel", "arbitrary")),
        name="rglru",
    )(proj, proj, conv_w, conv_b, w_a, b_a, w_x, b_x, lam)


def _mlstm_kernel(mx_ref, mo_ref, gif_ref, cw_ref, cb_ref, wq_ref, wk_ref, wv_ref, bif_ref,
                  ng_ref, skip_ref, o_ref, xbuf_ref, c_ref, n_ref, m_ref):
    L, D = M_CHUNK, M_HEAD_DIM

    @pl.when(pl.program_id(0) == 0)
    def _():
        xbuf_ref[pl.ds(0, SUBLANES), :] = jnp.zeros((SUBLANES, M_WIDTH), F32)
        c_ref[...] = jnp.zeros_like(c_ref)
        n_ref[...] = jnp.zeros_like(n_ref)
        m_ref[...] = jnp.zeros_like(m_ref)

    ux = mx_ref[...]
    conv = _causal_conv(ux, xbuf_ref, cw_ref[...], cb_ref[...], L)
    xc = conv * jax.nn.sigmoid(conv)

    gt = (gif_ref[...] + bif_ref[...]).T
    log_i = gt[0:M_HEADS, :]
    log_f = -_softplus(-gt[M_HEADS:2 * M_HEADS, :])
    lane = lax.broadcasted_iota(jnp.int32, (M_HEADS, L), 1)
    b = log_f
    k = 1
    while k < L:
        b = b + jnp.where(lane >= k, pltpu.roll(b, k, axis=1), 0.0)
        k *= 2
    g = b[:, L - 1:L]
    r = log_i - b
    a = g + r
    m_loc = jnp.max(a, axis=1, keepdims=True)
    w = jnp.exp(a - m_loc)
    m_prev = m_ref[:, 0:1]
    m_new = jnp.maximum(g + m_prev, m_loc)
    s_old = jnp.exp(g + m_prev - m_new)
    s_loc = jnp.exp(m_loc - m_new)
    m_ref[...] = jnp.broadcast_to(m_new, m_ref.shape)

    cols = jnp.concatenate([b, r, w, jnp.zeros((LANES - 3 * M_HEADS, L), F32)], axis=0).T

    tri = (lax.broadcasted_iota(jnp.int32, (L, L), 0)
           >= lax.broadcasted_iota(jnp.int32, (L, L), 1))
    nt_dims = (((1,), (1,)), ((), ()))

    for h in range(M_HEADS):
        sl = slice(h * D, (h + 1) * D)
        xch = xc[:, sl]
        xcb = xch.astype(BF16)
        q = jnp.dot(xcb, wq_ref[h], preferred_element_type=F32)
        kk = jnp.dot(xcb, wk_ref[h], preferred_element_type=F32) * (M_HEAD_DIM ** -0.5)
        v = jnp.dot(ux[:, sl].astype(BF16), wv_ref[h], preferred_element_type=F32)
        qb, kb = q.astype(BF16), kk.astype(BF16)

        b_col = cols[:, h:h + 1]
        w_col = cols[:, 2 * M_HEADS + h:2 * M_HEADS + h + 1]
        r_row = r[h:h + 1, :]
        m_prev_h = m_prev[h:h + 1, :]
        s_old_h = s_old[h:h + 1, :]
        s_loc_h = s_loc[h:h + 1, :]

        log_d = jnp.where(tri, b_col + r_row, -jnp.inf)
        log_inter = b_col + m_prev_h
        m_t = jnp.maximum(log_inter, jnp.max(log_d, axis=1, keepdims=True))
        scores = (lax.dot_general(qb, kb, nt_dims, preferred_element_type=F32)
                  * jnp.exp(log_d - m_t))
        inter_w = jnp.exp(log_inter - m_t)
        c_prev = c_ref[h]
        n_prev = n_ref[h]
        num = (jnp.dot(scores.astype(BF16), v.astype(BF16), preferred_element_type=F32)
               + inter_w * jnp.dot(qb, c_prev.astype(BF16), preferred_element_type=F32))
        den = (jnp.sum(scores, axis=1, keepdims=True)
               + inter_w * jnp.sum(q * n_prev, axis=1, keepdims=True))
        hh = num / jnp.maximum(jnp.abs(den), jnp.exp(-m_t))

        c_loc = jnp.dot(kk.T.astype(BF16), (w_col * v).astype(BF16), preferred_element_type=F32)
        c_ref[h] = s_old_h * c_prev + s_loc_h * c_loc
        n_loc = jnp.sum(w_col * kk, axis=0, keepdims=True)
        n_ref[h] = s_old_h * n_prev + s_loc_h * n_loc

        mu = jnp.mean(hh, axis=1, keepdims=True)
        cen = hh - mu
        var = jnp.mean(cen * cen, axis=1, keepdims=True)
        y = cen * lax.rsqrt(var + NORM_EPS) * ng_ref[:, sl] + skip_ref[:, sl] * xch
        o_ref[:, sl] = (jax.nn.sigmoid(mo_ref[:, sl]) * y).astype(o_ref.dtype)


def _mlstm(proj, gif, conv_w, conv_b, w_q, w_k, w_v, b_if, norm_g, skip):
    s = proj.shape[0]
    L = M_CHUNK
    full2 = lambda c: (0, 0)
    full3 = lambda c: (0, 0, 0)
    return pl.pallas_call(
        _mlstm_kernel,
        grid=(s // L,),
        in_specs=[
            pl.BlockSpec((L, M_WIDTH), lambda c: (c, COL_M_X)),
            pl.BlockSpec((L, M_WIDTH), lambda c: (c, COL_M_O)),
            pl.BlockSpec((L, LANES), lambda c: (c, 0)),
            pl.BlockSpec((CONV_WIDTH, M_WIDTH), full2),
            pl.BlockSpec((1, M_WIDTH), full2),
            pl.BlockSpec((M_HEADS, M_HEAD_DIM, M_HEAD_DIM), full3),
            pl.BlockSpec((M_HEADS, M_HEAD_DIM, M_HEAD_DIM), full3),
            pl.BlockSpec((M_HEADS, M_HEAD_DIM, M_HEAD_DIM), full3),
            pl.BlockSpec((1, LANES), full2),
            pl.BlockSpec((1, M_WIDTH), full2),
            pl.BlockSpec((1, M_WIDTH), full2),
        ],
        out_specs=pl.BlockSpec((L, M_WIDTH), lambda c: (c, 0)),
        out_shape=jax.ShapeDtypeStruct((s, M_WIDTH), BF16),
        scratch_shapes=[
            pltpu.VMEM((L + SUBLANES, M_WIDTH), F32),
            pltpu.VMEM((M_HEADS, M_HEAD_DIM, M_HEAD_DIM), F32),
            pltpu.VMEM((M_HEADS, 1, M_HEAD_DIM), F32),
            pltpu.VMEM((M_HEADS, LANES), F32),
        ],
        compiler_params=_compiler_params(("arbitrary",)),
        name="mlstm",
    )(proj, proj, gif, conv_w, conv_b, w_q, w_k, w_v, b_if, norm_g, skip)


def _merge_kernel(yrg_ref, ym_ref, grg_ref, gm_ref, wrg_ref, wm_ref, wout_ref, h_ref,
                  o_ref, acc_ref):
    j = pl.program_id(1)

    @pl.when(j == 0)
    def _():
        acc_ref[...] = jnp.zeros_like(acc_ref)

    merged = (jax.nn.sigmoid(grg_ref[...])
              * jnp.dot(yrg_ref[...], wrg_ref[...], preferred_element_type=F32)
              + jax.nn.sigmoid(gm_ref[...])
              * jnp.dot(ym_ref[...], wm_ref[...], preferred_element_type=F32))
    acc_ref[...] += jnp.dot(merged.astype(BF16), wout_ref[...], preferred_element_type=F32)

    @pl.when(j == pl.num_programs(1) - 1)
    def _():
        o_ref[...] = h_ref[...] + acc_ref[...]


def _merge(y_rg, y_m, proj, w_rg, w_m, w_out, h1):
    s, d = h1.shape
    tm, tn = MERGE_TM, MERGE_TN
    nj = d // tn
    return pl.pallas_call(
        _merge_kernel,
        grid=(s // tm, nj),
        in_specs=[
            pl.BlockSpec((tm, RG_WIDTH), lambda i, j: (i, 0)),
            pl.BlockSpec((tm, M_WIDTH), lambda i, j: (i, 0)),
            pl.BlockSpec((tm, tn), lambda i, j: (i, COL_GATE_RG * nj + j)),
            pl.BlockSpec((tm, tn), lambda i, j: (i, COL_GATE_M * nj + j)),
            pl.BlockSpec((RG_WIDTH, tn), lambda i, j: (0, j)),
            pl.BlockSpec((M_WIDTH, tn), lambda i, j: (0, j)),
            pl.BlockSpec((tn, d), lambda i, j: (j, 0)),
            pl.BlockSpec((tm, d), lambda i, j: (i, 0)),
        ],
        out_specs=pl.BlockSpec((tm, d), lambda i, j: (i, 0)),
        out_shape=jax.ShapeDtypeStruct((s, d), F32),
        scratch_shapes=[pltpu.VMEM((tm, d), F32)],
        compiler_params=_compiler_params(("parallel", "arbitrary")),
        name="merge",
    )(y_rg, y_m, proj, proj, w_rg, w_m, w_out, h1)


def kernel(x, ffn1_norm, ffn1_w_gate, ffn1_w_up, ffn1_w_down, mix_norm, w_in, rg_conv_w, rg_conv_b, rg_w_a, rg_b_a, rg_w_x, rg_b_x, rg_lambda, m_conv_w, m_conv_b, m_w_q, m_w_k, m_w_v, m_b_i, m_b_f, m_norm, m_skip, w_proj_rg, w_proj_m, w_out, ffn2_norm, ffn2_w_gate, ffn2_w_up, ffn2_w_down, final_norm):
    bsz, s, d = x.shape
    assert (bsz, s, d) == (1, SEQ, D_MODEL) and ffn1_norm.shape[0] == 1
    bf = lambda w: w.astype(BF16)
    h = x[0]

    h = _ffn(h, ffn1_norm, bf(ffn1_w_gate[0]), bf(ffn1_w_up[0]), bf(ffn1_w_down[0]))

    w_in0 = w_in[0]
    w_if = jnp.pad(w_in0[:, N_WIDE:], ((0, 0), (0, LANES - 2 * M_HEADS)))
    proj, gif = _inproj(h, mix_norm, bf(w_in0[:, :N_WIDE]), bf(w_if))

    y_rg = _rglru(proj, rg_conv_w[0], rg_conv_b, bf(rg_w_a[0]),
                  rg_b_a[0][:, None, :], bf(rg_w_x[0]), rg_b_x[0][:, None, :], rg_lambda)

    b_if = jnp.pad(jnp.concatenate([m_b_i, m_b_f], axis=1), ((0, 0), (0, LANES - 2 * M_HEADS)))
    y_m = _mlstm(proj, gif, m_conv_w[0], m_conv_b, bf(m_w_q[0]), bf(m_w_k[0]), bf(m_w_v[0]),
                 b_if, m_norm, m_skip)

    h = _merge(y_rg, y_m, proj, bf(w_proj_rg[0]), bf(w_proj_m[0]), bf(w_out[0]), h)

    h = _ffn(h, ffn2_norm, bf(ffn2_w_gate[0]), bf(ffn2_w_up[0]), bf(ffn2_w_down[0]),
             final_g=final_norm[None, :])
    return h[None]
```

```python
import functools

import jax
import jax.numpy as jnp
from jax import lax
from jax.experimental import pallas as pl
from jax.experimental.pallas import tpu as pltpu

D_MODEL = 2048
SEQ = 8192
D_FF = 5632
CONV_WIDTH = 4
RG_WIDTH = 2048
RG_BLOCKS = 8
RG_BLOCK = RG_WIDTH // RG_BLOCKS
RG_C = 8.0
M_HEADS = 8
M_HEAD_DIM = 256
M_WIDTH = M_HEADS * M_HEAD_DIM
M_CHUNK = 128
NORM_EPS = 1e-6

LANES = 128
SUBLANES = 8
V7X_VMEM_BYTES = 64 * 1024 * 1024
VMEM_LIMIT_BYTES = (V7X_VMEM_BYTES // 16) * 15

COL_RG_X, COL_RG_GATE, COL_M_X, COL_M_O, COL_GATE_RG, COL_GATE_M = range(6)
N_WIDE = 6 * D_MODEL

FFN_TM, FFN_TF = 1024, 256
INPROJ_TM, INPROJ_TN = 1024, 512
RG_TS = 512
MERGE_TM, MERGE_TN = 512, 512

BF16 = jnp.bfloat16
F32 = jnp.float32


def _rms_norm(x, g):
    ms = jnp.mean(x * x, axis=-1, keepdims=True)
    return x * lax.rsqrt(ms + NORM_EPS) * g


def _softplus(z):
    return jnp.maximum(z, 0.0) + jnp.log1p(jnp.exp(-jnp.abs(z)))


def _compiler_params(semantics):
    return pltpu.CompilerParams(dimension_semantics=semantics,
                                vmem_limit_bytes=VMEM_LIMIT_BYTES)


def _ffn_kernel(*refs, final_norm):
    if final_norm:
        x_ref, g_ref, wg_ref, wu_ref, wd_ref, fg_ref, o_ref, u_ref = refs
    else:
        x_ref, g_ref, wg_ref, wu_ref, wd_ref, o_ref, u_ref = refs
    j = pl.program_id(1)

    @pl.when(j == 0)
    def _():
        u_ref[...] = _rms_norm(x_ref[...], g_ref[...]).astype(BF16)
        o_ref[...] = jnp.zeros_like(o_ref)

    u = u_ref[...]
    gate = jnp.dot(u, wg_ref[...].astype(BF16), preferred_element_type=F32)
    up = jnp.dot(u, wu_ref[...].astype(BF16), preferred_element_type=F32)
    act = (gate * jax.nn.sigmoid(gate) * up).astype(BF16)
    o_ref[...] += jnp.dot(act, wd_ref[...].astype(BF16), preferred_element_type=F32)

    @pl.when(j == pl.num_programs(1) - 1)
    def _():
        h = x_ref[...] + 0.5 * o_ref[...]
        if final_norm:
            h = _rms_norm(h, fg_ref[...])
        o_ref[...] = h


def _ffn(x, norm_g, wg, wu, wd, final_g=None):
    s, d = x.shape
    f = wg.shape[1]
    tm, tf = FFN_TM, FFN_TF
    final_norm = final_g is not None
    in_specs = [
        pl.BlockSpec((tm, d), lambda i, j: (i, 0)),
        pl.BlockSpec((1, d), lambda i, j: (0, 0)),
        pl.BlockSpec((d, tf), lambda i, j: (0, j)),
        pl.BlockSpec((d, tf), lambda i, j: (0, j)),
        pl.BlockSpec((tf, d), lambda i, j: (j, 0)),
    ]
    args = [x, norm_g, wg, wu, wd]
    if final_norm:
        in_specs.append(pl.BlockSpec((1, d), lambda i, j: (0, 0)))
        args.append(final_g)
    return pl.pallas_call(
        functools.partial(_ffn_kernel, final_norm=final_norm),
        grid=(s // tm, f // tf),
        in_specs=in_specs,
        out_specs=pl.BlockSpec((tm, d), lambda i, j: (i, 0)),
        out_shape=jax.ShapeDtypeStruct((s, d), F32),
        scratch_shapes=[pltpu.VMEM((tm, d), BF16)],
        compiler_params=_compiler_params(("parallel", "arbitrary")),
        name="ffn_final" if final_norm else "ffn",
    )(*args)


def _inproj_kernel(h_ref, g_ref, w_ref, wif_ref, o_ref, gif_ref, u_ref):
    j = pl.program_id(1)

    @pl.when(j == 0)
    def _():
        u = _rms_norm(h_ref[...], g_ref[...]).astype(BF16)
        u_ref[...] = u
        gif_ref[...] = jnp.dot(u, wif_ref[...], preferred_element_type=F32)

    o_ref[...] = jnp.dot(u_ref[...], w_ref[...].astype(BF16), preferred_element_type=F32)


def _inproj(h, norm_g, w_in, w_if):
    s, d = h.shape
    n = N_WIDE
    tm, tn = INPROJ_TM, INPROJ_TN
    return pl.pallas_call(
        _inproj_kernel,
        grid=(s // tm, n // tn),
        in_specs=[
            pl.BlockSpec((tm, d), lambda i, j: (i, 0)),
            pl.BlockSpec((1, d), lambda i, j: (0, 0)),
            pl.BlockSpec((d, tn), lambda i, j: (0, j)),
            pl.BlockSpec((d, LANES), lambda i, j: (0, 0)),
        ],
        out_specs=[
            pl.BlockSpec((tm, tn), lambda i, j: (i, j)),
            pl.BlockSpec((tm, LANES), lambda i, j: (i, 0)),
        ],
        out_shape=[jax.ShapeDtypeStruct((s, n), F32),
                   jax.ShapeDtypeStruct((s, LANES), F32)],
        scratch_shapes=[pltpu.VMEM((tm, d), BF16)],
        compiler_params=_compiler_params(("parallel", "arbitrary")),
        name="in_proj",
    )(h, norm_g, w_in, w_if)


def _causal_conv(x, xbuf_ref, cw, cb, rows):
    xbuf_ref[pl.ds(SUBLANES, rows), :] = x
    out = cb + cw[CONV_WIDTH - 1:CONV_WIDTH, :] * x
    for back in range(1, CONV_WIDTH):
        tap = CONV_WIDTH - 1 - back
        out = out + cw[tap:tap + 1, :] * xbuf_ref[pl.ds(SUBLANES - back, rows), :]
    xbuf_ref[pl.ds(0, SUBLANES), :] = xbuf_ref[pl.ds(rows, SUBLANES), :]
    return out


def _rglru_kernel(x_ref, gate_ref, cw_ref, cb_ref, wa_ref, ba_ref, wx_ref, bx_ref, lam_ref,
                  o_ref, xbuf_ref, a_ref, b_ref, h_ref, carry_ref):
    ts = x_ref.shape[0]

    @pl.when(pl.program_id(1) == 0)
    def _():
        xbuf_ref[pl.ds(0, SUBLANES), :] = jnp.zeros((SUBLANES, RG_BLOCK), F32)
        carry_ref[...] = jnp.zeros_like(carry_ref)

    xc = _causal_conv(x_ref[...], xbuf_ref, cw_ref[...], cb_ref[...], ts)
    xcb = xc.astype(BF16)
    r = jax.nn.sigmoid(jnp.dot(xcb, wa_ref[0], preferred_element_type=F32) + ba_ref[0])
    i = jax.nn.sigmoid(jnp.dot(xcb, wx_ref[0], preferred_element_type=F32) + bx_ref[0])
    log_a = -RG_C * r * _softplus(-lam_ref[...])
    a = jnp.exp(log_a)
    a_ref[...] = a
    b_ref[...] = jnp.sqrt(-jnp.tanh(log_a) * (a * a + 1.0)) * (i * xc)

    row = lax.broadcasted_iota(jnp.int32, (SUBLANES, RG_BLOCK), 0)

    def group(gi, carry):
        r0 = pl.multiple_of(gi * SUBLANES, SUBLANES)
        a = a_ref[pl.ds(r0, SUBLANES), :]
        b = b_ref[pl.ds(r0, SUBLANES), :]
        for k in (1, 2, 4):
            keep = row >= k
            a_prev = pltpu.roll(a, k, axis=0)
            b_prev = pltpu.roll(b, k, axis=0)
            b = jnp.where(keep, a * b_prev + b, b)
            a = jnp.where(keep, a * a_prev, a)
        h = b + a * carry
        h_ref[pl.ds(r0, SUBLANES), :] = h
        return h[SUBLANES - 1:SUBLANES, :]

    carry_ref[...] = lax.fori_loop(0, ts // SUBLANES, group, carry_ref[...], unroll=4)
    o_ref[...] = (h_ref[...] * jax.nn.gelu(gate_ref[...])).astype(o_ref.dtype)


def _rglru(proj, conv_w, conv_b, w_a, b_a, w_x, b_x, lam):
    s = proj.shape[0]
    ts = RG_TS
    blk = RG_BLOCK
    vec = lambda n, t: (0, n)
    return pl.pallas_call(
        _rglru_kernel,
        grid=(RG_BLOCKS, s // ts),
        in_specs=[
            pl.BlockSpec((ts, blk), lambda n, t: (t, COL_RG_X * RG_BLOCKS + n)),
            pl.BlockSpec((ts, blk), lambda n, t: (t, COL_RG_GATE * RG_BLOCKS + n)),
            pl.BlockSpec((CONV_WIDTH, blk), vec),
            pl.BlockSpec((1, blk), vec),
            pl.BlockSpec((1, blk, blk), lambda n, t: (n, 0, 0)),
            pl.BlockSpec((1, 1, blk), lambda n, t: (n, 0, 0)),
            pl.BlockSpec((1, blk, blk), lambda n, t: (n, 0, 0)),
            pl.BlockSpec((1, 1, blk), lambda n, t: (n, 0, 0)),
            pl.BlockSpec((1, blk), vec),
        ],
        out_specs=pl.BlockSpec((ts, blk), lambda n, t: (t, n)),
        out_shape=jax.ShapeDtypeStruct((s, RG_WIDTH), BF16),
        scratch_shapes=[
            pltpu.VMEM((ts + SUBLANES, blk), F32),
            pltpu.VMEM((ts, blk), F32),
            pltpu.VMEM((ts, blk), F32),
            pltpu.VMEM((ts, blk), F32),
            pltpu.VMEM((1, blk), F32),
        ],
        compiler_params=_compiler_params(("parallel", "arbitrary")),
        name="rglru",
    )(proj, proj, conv_w, conv_b, w_a, b_a, w_x, b_x, lam)


def _mlstm_kernel(mx_ref, mo_ref, gif_ref, cw_ref, cb_ref, wq_ref, wk_ref, wv_ref, bif_ref,
                  ng_ref, skip_ref, o_ref, xbuf_ref, c_ref, n_ref, m_ref):
    L, D = M_CHUNK, M_HEAD_DIM

    @pl.when(pl.program_id(0) == 0)
    def _():
        xbuf_ref[pl.ds(0, SUBLANES), :] = jnp.zeros((SUBLANES, M_WIDTH), F32)
        c_ref[...] = jnp.zeros_like(c_ref)
        n_ref[...] = jnp.zeros_like(n_ref)
        m_ref[...] = jnp.zeros_like(m_ref)

    ux = mx_ref[...]
    conv = _causal_conv(ux, xbuf_ref, cw_ref[...], cb_ref[...], L)
    xc = conv * jax.nn.sigmoid(conv)

    gt = (gif_ref[...] + bif_ref[...]).T
    log_i = gt[0:M_HEADS, :]
    log_f = -_softplus(-gt[M_HEADS:2 * M_HEADS, :])
    lane = lax.broadcasted_iota(jnp.int32, (M_HEADS, L), 1)
    b = log_f
    k = 1
    while k < L:
        b = b + jnp.where(lane >= k, pltpu.roll(b, k, axis=1), 0.0)
        k *= 2
    g = b[:, L - 1:L]
    r = log_i - b
    a = g + r
    m_loc = jnp.max(a, axis=1, keepdims=True)
    w = jnp.exp(a - m_loc)
    m_prev = m_ref[:, 0:1]
    m_new = jnp.maximum(g + m_prev, m_loc)
    s_old = jnp.exp(g + m_prev - m_new)
    s_loc = jnp.exp(m_loc - m_new)
    m_ref[...] = jnp.broadcast_to(m_new, m_ref.shape)

    cols = jnp.concatenate([b, r, w, jnp.zeros((LANES - 3 * M_HEADS, L), F32)], axis=0).T

    tri = (lax.broadcasted_iota(jnp.int32, (L, L), 0)
           >= lax.broadcasted_iota(jnp.int32, (L, L), 1))
    nt_dims = (((1,), (1,)), ((), ()))

    heads = range(M_HEADS)
    sls = [slice(h * D, (h + 1) * D) for h in heads]

    xcb = [xc[:, sl].astype(BF16) for sl in sls]
    q = [jnp.dot(xcb[h], wq_ref[h], preferred_element_type=F32) for h in heads]
    kk = [jnp.dot(xcb[h], wk_ref[h], preferred_element_type=F32) * (M_HEAD_DIM ** -0.5)
          for h in heads]
    v = [jnp.dot(ux[:, sls[h]].astype(BF16), wv_ref[h], preferred_element_type=F32)
         for h in heads]
    qb = [x.astype(BF16) for x in q]
    kb = [x.astype(BF16) for x in kk]
    qk = [lax.dot_general(qb[h], kb[h], nt_dims, preferred_element_type=F32) for h in heads]

    b_col = [cols[:, h:h + 1] for h in heads]
    w_col = [cols[:, 2 * M_HEADS + h:2 * M_HEADS + h + 1] for h in heads]
    c_prev = [c_ref[h] for h in heads]
    n_prev = [n_ref[h] for h in heads]
    qc = [jnp.dot(qb[h], c_prev[h].astype(BF16), preferred_element_type=F32) for h in heads]
    c_loc = [jnp.dot(kk[h].T.astype(BF16), (w_col[h] * v[h]).astype(BF16),
                     preferred_element_type=F32) for h in heads]

    scores, inter_w, m_t = [], [], []
    for h in heads:
        log_d = jnp.where(tri, b_col[h] + r[h:h + 1, :], -jnp.inf)
        log_inter = b_col[h] + m_prev[h:h + 1, :]
        m_t.append(jnp.maximum(log_inter, jnp.max(log_d, axis=1, keepdims=True)))
        scores.append(qk[h] * jnp.exp(log_d - m_t[h]))
        inter_w.append(jnp.exp(log_inter - m_t[h]))
    sv = [jnp.dot(scores[h].astype(BF16), v[h].astype(BF16), preferred_element_type=F32)
          for h in heads]

    for h in heads:
        s_old_h = s_old[h:h + 1, :]
        s_loc_h = s_loc[h:h + 1, :]
        c_ref[h] = s_old_h * c_prev[h] + s_loc_h * c_loc[h]
        n_loc = jnp.sum(w_col[h] * kk[h], axis=0, keepdims=True)
        n_ref[h] = s_old_h * n_prev[h] + s_loc_h * n_loc

    for h in heads:
        sl = sls[h]
        num = sv[h] + inter_w[h] * qc[h]
        den = (jnp.sum(scores[h], axis=1, keepdims=True)
               + inter_w[h] * jnp.sum(q[h] * n_prev[h], axis=1, keepdims=True))
        hh = num / jnp.maximum(jnp.abs(den), jnp.exp(-m_t[h]))
        mu = jnp.mean(hh, axis=1, keepdims=True)
        cen = hh - mu
        var = jnp.mean(cen * cen, axis=1, keepdims=True)
        y = cen * lax.rsqrt(var + NORM_EPS) * ng_ref[:, sl] + skip_ref[:, sl] * xc[:, sl]
        o_ref[:, sl] = (jax.nn.sigmoid(mo_ref[:, sl]) * y).astype(o_ref.dtype)


def _mlstm(proj, gif, conv_w, conv_b, w_q, w_k, w_v, b_if, norm_g, skip):
    s = proj.shape[0]
    L = M_CHUNK
    full2 = lambda c: (0, 0)
    full3 = lambda c: (0, 0, 0)
    return pl.pallas_call(
        _mlstm_kernel,
        grid=(s // L,),
        in_specs=[
            pl.BlockSpec((L, M_WIDTH), lambda c: (c, COL_M_X)),
            pl.BlockSpec((L, M_WIDTH), lambda c: (c, COL_M_O)),
            pl.BlockSpec((L, LANES), lambda c: (c, 0)),
            pl.BlockSpec((CONV_WIDTH, M_WIDTH), full2),
            pl.BlockSpec((1, M_WIDTH), full2),
            pl.BlockSpec((M_HEADS, M_HEAD_DIM, M_HEAD_DIM), full3),
            pl.BlockSpec((M_HEADS, M_HEAD_DIM, M_HEAD_DIM), full3),
            pl.BlockSpec((M_HEADS, M_HEAD_DIM, M_HEAD_DIM), full3),
            pl.BlockSpec((1, LANES), full2),
            pl.BlockSpec((1, M_WIDTH), full2),
            pl.BlockSpec((1, M_WIDTH), full2),
        ],
        out_specs=pl.BlockSpec((L, M_WIDTH), lambda c: (c, 0)),
        out_shape=jax.ShapeDtypeStruct((s, M_WIDTH), BF16),
        scratch_shapes=[
            pltpu.VMEM((L + SUBLANES, M_WIDTH), F32),
            pltpu.VMEM((M_HEADS, M_HEAD_DIM, M_HEAD_DIM), F32),
            pltpu.VMEM((M_HEADS, 1, M_HEAD_DIM), F32),
            pltpu.VMEM((M_HEADS, LANES), F32),
        ],
        compiler_params=_compiler_params(("arbitrary",)),
        name="mlstm",
    )(proj, proj, gif, conv_w, conv_b, w_q, w_k, w_v, b_if, norm_g, skip)


def _merge_kernel(yrg_ref, ym_ref, grg_ref, gm_ref, wrg_ref, wm_ref, wout_ref, h_ref,
                  o_ref, acc_ref):
    j = pl.program_id(1)

    @pl.when(j == 0)
    def _():
        acc_ref[...] = jnp.zeros_like(acc_ref)

    merged = (jax.nn.sigmoid(grg_ref[...])
              * jnp.dot(yrg_ref[...], wrg_ref[...], preferred_element_type=F32)
              + jax.nn.sigmoid(gm_ref[...])
              * jnp.dot(ym_ref[...], wm_ref[...], preferred_element_type=F32))
    acc_ref[...] += jnp.dot(merged.astype(BF16), wout_ref[...], preferred_element_type=F32)

    @pl.when(j == pl.num_programs(1) - 1)
    def _():
        o_ref[...] = h_ref[...] + acc_ref[...]


def _merge(y_rg, y_m, proj, w_rg, w_m, w_out, h1):
    s, d = h1.shape
    tm, tn = MERGE_TM, MERGE_TN
    nj = d // tn
    return pl.pallas_call(
        _merge_kernel,
        grid=(s // tm, nj),
        in_specs=[
            pl.BlockSpec((tm, RG_WIDTH), lambda i, j: (i, 0)),
            pl.BlockSpec((tm, M_WIDTH), lambda i, j: (i, 0)),
            pl.BlockSpec((tm, tn), lambda i, j: (i, COL_GATE_RG * nj + j)),
            pl.BlockSpec((tm, tn), lambda i, j: (i, COL_GATE_M * nj + j)),
            pl.BlockSpec((RG_WIDTH, tn), lambda i, j: (0, j)),
            pl.BlockSpec((M_WIDTH, tn), lambda i, j: (0, j)),
            pl.BlockSpec((tn, d), lambda i, j: (j, 0)),
            pl.BlockSpec((tm, d), lambda i, j: (i, 0)),
        ],
        out_specs=pl.BlockSpec((tm, d), lambda i, j: (i, 0)),
        out_shape=jax.ShapeDtypeStruct((s, d), F32),
        scratch_shapes=[pltpu.VMEM((tm, d), F32)],
        compiler_params=_compiler_params(("parallel", "arbitrary")),
        name="merge",
    )(y_rg, y_m, proj, proj, w_rg, w_m, w_out, h1)


def kernel(x, ffn1_norm, ffn1_w_gate, ffn1_w_up, ffn1_w_down, mix_norm, w_in, rg_conv_w, rg_conv_b, rg_w_a, rg_b_a, rg_w_x, rg_b_x, rg_lambda, m_conv_w, m_conv_b, m_w_q, m_w_k, m_w_v, m_b_i, m_b_f, m_norm, m_skip, w_proj_rg, w_proj_m, w_out, ffn2_norm, ffn2_w_gate, ffn2_w_up, ffn2_w_down, final_norm):
    bsz, s, d = x.shape
    assert (bsz, s, d) == (1, SEQ, D_MODEL) and ffn1_norm.shape[0] == 1
    bf = lambda w: w.astype(BF16)
    h = x[0]

    h = _ffn(h, ffn1_norm, ffn1_w_gate[0], ffn1_w_up[0], ffn1_w_down[0])

    w_in0 = w_in[0]
    w_if = jnp.pad(w_in0[:, N_WIDE:], ((0, 0), (0, LANES - 2 * M_HEADS)))
    proj, gif = _inproj(h, mix_norm, w_in0, bf(w_if))

    y_rg = _rglru(proj, rg_conv_w[0], rg_conv_b, bf(rg_w_a[0]),
                  rg_b_a[0][:, None, :], bf(rg_w_x[0]), rg_b_x[0][:, None, :], rg_lambda)

    b_if = jnp.pad(jnp.concatenate([m_b_i, m_b_f], axis=1), ((0, 0), (0, LANES - 2 * M_HEADS)))
    y_m = _mlstm(proj, gif, m_conv_w[0], m_conv_b, bf(m_w_q[0]), bf(m_w_k[0]), bf(m_w_v[0]),
                 b_if, m_norm, m_skip)

    h = _merge(y_rg, y_m, proj, bf(w_proj_rg[0]), bf(w_proj_m[0]), bf(w_out[0]), h)

    h = _ffn(h, ffn2_norm, ffn2_w_gate[0], ffn2_w_up[0], ffn2_w_down[0],
             final_g=final_norm[None, :])
    return h[None]
```

```python
import functools

import jax
import jax.numpy as jnp
from jax import lax
from jax.experimental import pallas as pl
from jax.experimental.pallas import tpu as pltpu

D_MODEL = 2048
SEQ = 8192
D_FF = 5632
CONV_WIDTH = 4
RG_WIDTH = 2048
RG_BLOCKS = 8
RG_BLOCK = RG_WIDTH // RG_BLOCKS
RG_C = 8.0
M_HEADS = 8
M_HEAD_DIM = 256
M_WIDTH = M_HEADS * M_HEAD_DIM
M_CHUNK = 128
NORM_EPS = 1e-6

LANES = 128
SUBLANES = 8
V7X_VMEM_BYTES = 64 * 1024 * 1024
VMEM_LIMIT_BYTES = (V7X_VMEM_BYTES // 16) * 15

COL_RG_X, COL_RG_GATE, COL_M_X, COL_M_O, COL_GATE_RG, COL_GATE_M = range(6)
N_WIDE = 6 * D_MODEL

FFN_TM, FFN_TF = 1024, 256
FFN_ROW_CHUNK = 128
INPROJ_TM, INPROJ_TN = 2048, 512
RG_TS = 512
MERGE_TM = 256

BF16 = jnp.bfloat16
F32 = jnp.float32


def _rms_norm(x, g):
    ms = jnp.mean(x * x, axis=-1, keepdims=True)
    return x * lax.rsqrt(ms + NORM_EPS) * g


def _softplus(z):
    return jnp.maximum(z, 0.0) + jnp.log1p(jnp.exp(-jnp.abs(z)))


def _compiler_params(semantics):
    return pltpu.CompilerParams(dimension_semantics=semantics,
                                vmem_limit_bytes=VMEM_LIMIT_BYTES)


def _ffn_kernel(*refs, final_norm):
    if final_norm:
        x_ref, g_ref, wg_ref, wu_ref, wd_ref, eg_ref, o_ref, u_ref = refs
    else:
        x_ref, g_ref, wg_ref, wu_ref, wd_ref, eg_ref, o_ref, un_ref, u_ref = refs
    j = pl.program_id(1)

    row_chunks = [pl.ds(r, FFN_ROW_CHUNK) for r in range(0, x_ref.shape[0], FFN_ROW_CHUNK)]

    @pl.when(j == 0)
    def _():
        for rows in row_chunks:
            u_ref[rows, :] = _rms_norm(x_ref[rows, :], g_ref[...]).astype(BF16)
        o_ref[...] = jnp.zeros_like(o_ref)

    u = u_ref[...]
    gate = jnp.dot(u, wg_ref[...].astype(BF16), preferred_element_type=F32)
    up = jnp.dot(u, wu_ref[...].astype(BF16), preferred_element_type=F32)
    act = (gate * jax.nn.sigmoid(gate) * up).astype(BF16)
    o_ref[...] += jnp.dot(act, wd_ref[...].astype(BF16), preferred_element_type=F32)

    @pl.when(j == pl.num_programs(1) - 1)
    def _():
        for rows in row_chunks:
            h = x_ref[rows, :] + 0.5 * o_ref[rows, :]
            if final_norm:
                o_ref[rows, :] = _rms_norm(h, eg_ref[...])
            else:
                o_ref[rows, :] = h
                un_ref[rows, :] = _rms_norm(h, eg_ref[...]).astype(BF16)


def _ffn(x, norm_g, wg, wu, wd, extra_g, final_norm):
    s, d = x.shape
    f = wg.shape[1]
    tm, tf = FFN_TM, FFN_TF
    row_spec = pl.BlockSpec((tm, d), lambda i, j: (i, 0))
    gain_spec = pl.BlockSpec((1, d), lambda i, j: (0, 0))
    out_specs, out_shape = row_spec, jax.ShapeDtypeStruct((s, d), F32)
    if not final_norm:
        out_specs = [row_spec, row_spec]
        out_shape = [out_shape, jax.ShapeDtypeStruct((s, d), BF16)]
    return pl.pallas_call(
        functools.partial(_ffn_kernel, final_norm=final_norm),
        grid=(s // tm, f // tf),
        in_specs=[
            row_spec,
            gain_spec,
            pl.BlockSpec((d, tf), lambda i, j: (0, j)),
            pl.BlockSpec((d, tf), lambda i, j: (0, j)),
            pl.BlockSpec((tf, d), lambda i, j: (j, 0)),
            gain_spec,
        ],
        out_specs=out_specs,
        out_shape=out_shape,
        scratch_shapes=[pltpu.VMEM((tm, d), BF16)],
        compiler_params=_compiler_params(("parallel", "arbitrary")),
        name="ffn_final" if final_norm else "ffn",
    )(x, norm_g, wg, wu, wd, extra_g)


_NT_DIMS = (((1,), (1,)), ((), ()))


def _inproj_kernel(u_ref, wt_ref, wift_ref, o_ref, gif_ref):
    u = u_ref[...]

    @pl.when(pl.program_id(1) == 0)
    def _():
        gif_ref[...] = lax.dot_general(u, wift_ref[...], _NT_DIMS, preferred_element_type=F32)

    o_ref[...] = lax.dot_general(u, wt_ref[...].astype(BF16), _NT_DIMS,
                                 preferred_element_type=F32)


def _inproj(u, w_in_t, w_if_t):
    s, d = u.shape
    n = N_WIDE
    tm, tn = INPROJ_TM, INPROJ_TN
    return pl.pallas_call(
        _inproj_kernel,
        grid=(s // tm, n // tn),
        in_specs=[
            pl.BlockSpec((tm, d), lambda i, j: (i, 0)),
            pl.BlockSpec((tn, d), lambda i, j: (j, 0)),
            pl.BlockSpec((LANES, d), lambda i, j: (0, 0)),
        ],
        out_specs=[
            pl.BlockSpec((tm, tn), lambda i, j: (i, j)),
            pl.BlockSpec((tm, LANES), lambda i, j: (i, 0)),
        ],
        out_shape=[jax.ShapeDtypeStruct((s, n), F32),
                   jax.ShapeDtypeStruct((s, LANES), F32)],
        compiler_params=_compiler_params(("parallel", "arbitrary")),
        name="in_proj",
    )(u, w_in_t, w_if_t)


def _causal_conv(x, xbuf_ref, cw, cb, rows):
    xbuf_ref[pl.ds(SUBLANES, rows), :] = x
    out = cb + cw[CONV_WIDTH - 1:CONV_WIDTH, :] * x
    for back in range(1, CONV_WIDTH):
        tap = CONV_WIDTH - 1 - back
        out = out + cw[tap:tap + 1, :] * xbuf_ref[pl.ds(SUBLANES - back, rows), :]
    xbuf_ref[pl.ds(0, SUBLANES), :] = xbuf_ref[pl.ds(rows, SUBLANES), :]
    return out


def _rglru_kernel(x_ref, gate_ref, cw_ref, cb_ref, wa_ref, ba_ref, wx_ref, bx_ref, lam_ref,
                  o_ref, xbuf_ref, a_ref, b_ref, h_ref, carry_ref):
    ts = x_ref.shape[0]

    @pl.when(pl.program_id(1) == 0)
    def _():
        xbuf_ref[pl.ds(0, SUBLANES), :] = jnp.zeros((SUBLANES, RG_BLOCK), F32)
        carry_ref[...] = jnp.zeros_like(carry_ref)

    xc = _causal_conv(x_ref[...], xbuf_ref, cw_ref[...], cb_ref[...], ts)
    xcb = xc.astype(BF16)
    r = jax.nn.sigmoid(jnp.dot(xcb, wa_ref[0], preferred_element_type=F32) + ba_ref[0])
    i = jax.nn.sigmoid(jnp.dot(xcb, wx_ref[0], preferred_element_type=F32) + bx_ref[0])
    log_a = -RG_C * r * _softplus(-lam_ref[...])
    a = jnp.exp(log_a)
    a_ref[...] = a
    b_ref[...] = jnp.sqrt(-jnp.tanh(log_a) * (a * a + 1.0)) * (i * xc)

    row = lax.broadcasted_iota(jnp.int32, (SUBLANES, RG_BLOCK), 0)

    def group(gi, carry):
        r0 = pl.multiple_of(gi * SUBLANES, SUBLANES)
        a = a_ref[pl.ds(r0, SUBLANES), :]
        b = b_ref[pl.ds(r0, SUBLANES), :]
        for k in (1, 2, 4):
            keep = row >= k
            a_prev = pltpu.roll(a, k, axis=0)
            b_prev = pltpu.roll(b, k, axis=0)
            b = jnp.where(keep, a * b_prev + b, b)
            a = jnp.where(keep, a * a_prev, a)
        h = b + a * carry
        h_ref[pl.ds(r0, SUBLANES), :] = h
        return h[SUBLANES - 1:SUBLANES, :]

    carry_ref[...] = lax.fori_loop(0, ts // SUBLANES, group, carry_ref[...], unroll=4)
    o_ref[...] = (h_ref[...] * jax.nn.gelu(gate_ref[...])).astype(o_ref.dtype)


def _rglru(proj, conv_w, conv_b, w_a, b_a, w_x, b_x, lam):
    s = proj.shape[0]
    ts = RG_TS
    blk = RG_BLOCK
    vec = lambda n, t: (0, n)
    return pl.pallas_call(
        _rglru_kernel,
        grid=(RG_BLOCKS, s // ts),
        in_specs=[
            pl.BlockSpec((ts, blk), lambda n, t: (t, COL_RG_X * RG_BLOCKS + n)),
            pl.BlockSpec((ts, blk), lambda n, t: (t, COL_RG_GATE * RG_BLOCKS + n)),
            pl.BlockSpec((CONV_WIDTH, blk), vec),
            pl.BlockSpec((1, blk), vec),
            pl.BlockSpec((1, blk, blk), lambda n, t: (n, 0, 0)),
            pl.BlockSpec((1, 1, blk), lambda n, t: (n, 0, 0)),
            pl.BlockSpec((1, blk, blk), lambda n, t: (n, 0, 0)),
            pl.BlockSpec((1, 1, blk), lambda n, t: (n, 0, 0)),
            pl.BlockSpec((1, blk), vec),
        ],
        out_specs=pl.BlockSpec((ts, blk), lambda n, t: (t, n)),
        out_shape=jax.ShapeDtypeStruct((s, RG_WIDTH), BF16),
        scratch_shapes=[
            pltpu.VMEM((ts + SUBLANES, blk), F32),
            pltpu.VMEM((ts, blk), F32),
            pltpu.VMEM((ts, blk), F32),
            pltpu.VMEM((ts, blk), F32),
            pltpu.VMEM((1, blk), F32),
        ],
        compiler_params=_compiler_params(("parallel", "arbitrary")),
        name="rglru",
    )(proj, proj, conv_w, conv_b, w_a, b_a, w_x, b_x, lam)


def _mlstm_kernel(mx_ref, mo_ref, gif_ref, cw_ref, cb_ref, wq_ref, wk_ref, wv_ref, bif_ref,
                  ng_ref, skip_ref, o_ref, xbuf_ref, c_ref, n_ref, m_ref):
    L, D = M_CHUNK, M_HEAD_DIM

    @pl.when(pl.program_id(0) == 0)
    def _():
        xbuf_ref[pl.ds(0, SUBLANES), :] = jnp.zeros((SUBLANES, M_WIDTH), F32)
        c_ref[...] = jnp.zeros_like(c_ref)
        n_ref[...] = jnp.zeros_like(n_ref)
        m_ref[...] = jnp.zeros_like(m_ref)

    ux = mx_ref[...]
    conv = _causal_conv(ux, xbuf_ref, cw_ref[...], cb_ref[...], L)
    xc = conv * jax.nn.sigmoid(conv)

    gt = (gif_ref[...] + bif_ref[...]).T
    log_i = gt[0:M_HEADS, :]
    log_f = -_softplus(-gt[M_HEADS:2 * M_HEADS, :])
    lane = lax.broadcasted_iota(jnp.int32, (M_HEADS, L), 1)
    b = log_f
    k = 1
    while k < L:
        b = b + jnp.where(lane >= k, pltpu.roll(b, k, axis=1), 0.0)
        k *= 2
    g = b[:, L - 1:L]
    r = log_i - b
    a = g + r
    m_loc = jnp.max(a, axis=1, keepdims=True)
    w = jnp.exp(a - m_loc)
    m_prev = m_ref[:, 0:1]
    m_new = jnp.maximum(g + m_prev, m_loc)
    s_old = jnp.exp(g + m_prev - m_new)
    s_loc = jnp.exp(m_loc - m_new)
    m_ref[...] = jnp.broadcast_to(m_new, m_ref.shape)

    rmax = r
    k = 1
    while k < L:
        rmax = jnp.maximum(rmax, jnp.where(lane >= k, pltpu.roll(rmax, k, axis=1), -jnp.inf))
        k *= 2
    cm = jnp.maximum(m_prev, rmax)
    iw = jnp.exp(m_prev - cm)
    em = jnp.exp(-(b + cm))

    cols = jnp.concatenate([cm, w, iw, em, jnp.zeros((LANES - 4 * M_HEADS, L), F32)], axis=0).T

    tri = (lax.broadcasted_iota(jnp.int32, (L, L), 0)
           >= lax.broadcasted_iota(jnp.int32, (L, L), 1))
    nt_dims = (((1,), (1,)), ((), ()))

    heads = range(M_HEADS)
    sls = [slice(h * D, (h + 1) * D) for h in heads]

    xcb = [xc[:, sl].astype(BF16) for sl in sls]
    q = [jnp.dot(xcb[h], wq_ref[h], preferred_element_type=F32) for h in heads]
    kk = [jnp.dot(xcb[h], wk_ref[h], preferred_element_type=F32) * (M_HEAD_DIM ** -0.5)
          for h in heads]
    v = [jnp.dot(ux[:, sls[h]].astype(BF16), wv_ref[h], preferred_element_type=F32)
         for h in heads]
    qb = [x.astype(BF16) for x in q]
    kb = [x.astype(BF16) for x in kk]
    qk = [lax.dot_general(qb[h], kb[h], nt_dims, preferred_element_type=F32) for h in heads]

    col = lambda which, h: cols[:, which * M_HEADS + h:which * M_HEADS + h + 1]
    cm_col = [col(0, h) for h in heads]
    w_col = [col(1, h) for h in heads]
    inter_w = [col(2, h) for h in heads]
    em_col = [col(3, h) for h in heads]
    c_prev = [c_ref[h] for h in heads]
    n_prev = [n_ref[h] for h in heads]
    qc = [jnp.dot(qb[h], c_prev[h].astype(BF16), preferred_element_type=F32) for h in heads]
    c_loc = [jnp.dot(kk[h].T.astype(BF16), (w_col[h] * v[h]).astype(BF16),
                     preferred_element_type=F32) for h in heads]

    scores = [qk[h] * jnp.exp(jnp.where(tri, r[h:h + 1, :] - cm_col[h], -jnp.inf))
              for h in heads]
    sv =[jnp.dot(scores[h].astype(BF16), v[h].astype(BF16), preferred_element_type=F32)
          for h in heads]

    for h in heads:
        s_old_h = s_old[h:h + 1, :]
        s_loc_h = s_loc[h:h + 1, :]
        c_ref[h] = s_old_h * c_prev[h] + s_loc_h * c_loc[h]
        n_loc = jnp.sum(w_col[h] * kk[h], axis=0, keepdims=True)
        n_ref[h] = s_old_h * n_prev[h] + s_loc_h * n_loc

    for h in heads:
        sl = sls[h]
        num = sv[h] + inter_w[h] * qc[h]
        den = (jnp.sum(scores[h], axis=1, keepdims=True)
               + inter_w[h] * jnp.sum(q[h] * n_prev[h], axis=1, keepdims=True))
        hh = num * (1.0 / jnp.maximum(jnp.abs(den), em_col[h]))
        mu = jnp.mean(hh, axis=1, keepdims=True)
        cen = hh - mu
        var = jnp.mean(cen * cen, axis=1, keepdims=True)
        y = cen * lax.rsqrt(var + NORM_EPS) * ng_ref[:, sl] + skip_ref[:, sl] * xc[:, sl]
        o_ref[:, sl] = (jax.nn.sigmoid(mo_ref[:, sl]) * y).astype(o_ref.dtype)


def _mlstm(proj, gif, conv_w, conv_b, w_q, w_k, w_v, b_if, norm_g, skip):
    s = proj.shape[0]
    L = M_CHUNK
    full2 = lambda c: (0, 0)
    full3 = lambda c: (0, 0, 0)
    return pl.pallas_call(
        _mlstm_kernel,
        grid=(s // L,),
        in_specs=[
            pl.BlockSpec((L, M_WIDTH), lambda c: (c, COL_M_X)),
            pl.BlockSpec((L, M_WIDTH), lambda c: (c, COL_M_O)),
            pl.BlockSpec((L, LANES), lambda c: (c, 0)),
            pl.BlockSpec((CONV_WIDTH, M_WIDTH), full2),
            pl.BlockSpec((1, M_WIDTH), full2),
            pl.BlockSpec((M_HEADS, M_HEAD_DIM, M_HEAD_DIM), full3),
            pl.BlockSpec((M_HEADS, M_HEAD_DIM, M_HEAD_DIM), full3),
            pl.BlockSpec((M_HEADS, M_HEAD_DIM, M_HEAD_DIM), full3),
            pl.BlockSpec((1, LANES), full2),
            pl.BlockSpec((1, M_WIDTH), full2),
            pl.BlockSpec((1, M_WIDTH), full2),
        ],
        out_specs=pl.BlockSpec((L, M_WIDTH), lambda c: (c, 0)),
        out_shape=jax.ShapeDtypeStruct((s, M_WIDTH), BF16),
        scratch_shapes=[
            pltpu.VMEM((L + SUBLANES, M_WIDTH), F32),
            pltpu.VMEM((M_HEADS, M_HEAD_DIM, M_HEAD_DIM), F32),
            pltpu.VMEM((M_HEADS, 1, M_HEAD_DIM), F32),
            pltpu.VMEM((M_HEADS, LANES), F32),
        ],
        compiler_params=_compiler_params(("arbitrary",)),
        name="mlstm",
    )(proj, proj, gif, conv_w, conv_b, w_q, w_k, w_v, b_if, norm_g, skip)


def _merge_kernel(yrg_ref, ym_ref, grg_ref, gm_ref, wrg_ref, wm_ref, wout_ref, h_ref, o_ref):
    merged = (jax.nn.sigmoid(grg_ref[...])
              * jnp.dot(yrg_ref[...], wrg_ref[...], preferred_element_type=F32)
              + jax.nn.sigmoid(gm_ref[...])
              * jnp.dot(ym_ref[...], wm_ref[...], preferred_element_type=F32))
    o_ref[...] = h_ref[...] + jnp.dot(merged.astype(BF16), wout_ref[...],
                                      preferred_element_type=F32)


def _merge(y_rg, y_m, proj, w_rg, w_m, w_out, h1):
    s, d = h1.shape
    tm = MERGE_TM
    resident = lambda rows: pl.BlockSpec((rows, d), lambda i: (0, 0),
                                         pipeline_mode=pl.Buffered(1))
    return pl.pallas_call(
        _merge_kernel,
        grid=(s // tm,),
        in_specs=[
            pl.BlockSpec((tm, RG_WIDTH), lambda i: (i, 0)),
            pl.BlockSpec((tm, M_WIDTH), lambda i: (i, 0)),
            pl.BlockSpec((tm, d), lambda i: (i, COL_GATE_RG)),
            pl.BlockSpec((tm, d), lambda i: (i, COL_GATE_M)),
            resident(RG_WIDTH),
            resident(M_WIDTH),
            resident(d),
            pl.BlockSpec((tm, d), lambda i: (i, 0)),
        ],
        out_specs=pl.BlockSpec((tm, d), lambda i: (i, 0)),
        out_shape=jax.ShapeDtypeStruct((s, d), F32),
        compiler_params=_compiler_params(("parallel",)),
        name="merge",
    )(y_rg, y_m, proj, proj, w_rg, w_m, w_out, h1)


def kernel(x, ffn1_norm, ffn1_w_gate, ffn1_w_up, ffn1_w_down, mix_norm, w_in, rg_conv_w, rg_conv_b, rg_w_a, rg_b_a, rg_w_x, rg_b_x, rg_lambda, m_conv_w, m_conv_b, m_w_q, m_w_k, m_w_v, m_b_i, m_b_f, m_norm, m_skip, w_proj_rg, w_proj_m, w_out, ffn2_norm, ffn2_w_gate, ffn2_w_up, ffn2_w_down, final_norm):
    bsz, s, d = x.shape
    assert (bsz, s, d) == (1, SEQ, D_MODEL) and ffn1_norm.shape[0] == 1
    bf = lambda w: w.astype(BF16)
    h = x[0]

    h, u_mix = _ffn(h, ffn1_norm, ffn1_w_gate[0], ffn1_w_up[0], ffn1_w_down[0], mix_norm,
                    final_norm=False)

    w_in_t = jnp.swapaxes(w_in[0], 0, 1)
    w_if_t = jnp.pad(w_in_t[N_WIDE:], ((0, LANES - 2 * M_HEADS), (0, 0)))
    proj, gif = _inproj(u_mix, w_in_t, bf(w_if_t))

    y_rg = _rglru(proj, rg_conv_w[0], rg_conv_b, bf(rg_w_a[0]),
                  rg_b_a[0][:, None, :], bf(rg_w_x[0]), rg_b_x[0][:, None, :], rg_lambda)

    b_if = jnp.pad(jnp.concatenate([m_b_i, m_b_f], axis=1), ((0, 0), (0, LANES - 2 * M_HEADS)))
    y_m = _mlstm(proj, gif, m_conv_w[0], m_conv_b, bf(m_w_q[0]), bf(m_w_k[0]), bf(m_w_v[0]),
                 b_if, m_norm, m_skip)

    h = _merge(y_rg, y_m, proj, bf(w_proj_rg[0]), bf(w_proj_m[0]), bf(w_out[0]), h)

    h = _ffn(h, ffn2_norm, ffn2_w_gate[0], ffn2_w_up[0], ffn2_w_down[0], final_norm[None, :],
             final_norm=True)
    return h[None]
```

```python
import functools

import jax
import jax.numpy as jnp
from jax import lax
from jax.experimental import pallas as pl
from jax.experimental.pallas import tpu as pltpu

D_MODEL = 2048
SEQ = 8192
D_FF = 5632
CONV_WIDTH = 4
RG_WIDTH = 2048
RG_BLOCKS = 8
RG_BLOCK = RG_WIDTH // RG_BLOCKS
RG_C = 8.0
M_HEADS = 8
M_HEAD_DIM = 256
M_WIDTH = M_HEADS * M_HEAD_DIM
M_CHUNK = 128
NORM_EPS = 1e-6

LANES = 128
SUBLANES = 8
V7X_VMEM_BYTES = 64 * 1024 * 1024
VMEM_LIMIT_BYTES = (V7X_VMEM_BYTES // 16) * 15

ROW_RG = 0
ROW_M = 2 * RG_WIDTH
ROW_GATES = ROW_M + 2 * M_WIDTH
ROW_IF = ROW_GATES + 2 * D_MODEL

FFN_TM, FFN_TF = 1024, 256
FFN_ROW_CHUNK = 128
GATE_TM, GATE_TN = 2048, 512
MIX_TM = 256
M_PROJ_PIECE = 512
MERGE_TM = 256

BF16 = jnp.bfloat16
F32 = jnp.float32
_NT_DIMS = (((1,), (1,)), ((), ()))


def _rms_norm(x, g):
    ms = jnp.mean(x * x, axis=-1, keepdims=True)
    return x * lax.rsqrt(ms + NORM_EPS) * g


def _softplus(z):
    return jnp.maximum(z, 0.0) + jnp.log1p(jnp.exp(-jnp.abs(z)))


def _compiler_params(semantics):
    return pltpu.CompilerParams(dimension_semantics=semantics,
                                vmem_limit_bytes=VMEM_LIMIT_BYTES)


def _resident(shape):
    zeros = (0,) * len(shape)
    return pl.BlockSpec(shape, lambda *_: zeros, pipeline_mode=pl.Buffered(1))


def _ffn_kernel(*refs, final_norm):
    if final_norm:
        x_ref, g_ref, wg_ref, wu_ref, wd_ref, eg_ref, o_ref, u_ref = refs
    else:
        x_ref, g_ref, wg_ref, wu_ref, wd_ref, eg_ref, o_ref, un_ref, u_ref = refs
    j = pl.program_id(1)

    row_chunks = [pl.ds(r, FFN_ROW_CHUNK) for r in range(0, x_ref.shape[0], FFN_ROW_CHUNK)]

    @pl.when(j == 0)
    def _():
        for rows in row_chunks:
            u_ref[rows, :] = _rms_norm(x_ref[rows, :], g_ref[...]).astype(BF16)
        o_ref[...] = jnp.zeros_like(o_ref)

    u = u_ref[...]
    gate = jnp.dot(u, wg_ref[...].astype(BF16), preferred_element_type=F32)
    up = jnp.dot(u, wu_ref[...].astype(BF16), preferred_element_type=F32)
    act = (gate * jax.nn.sigmoid(gate) * up).astype(BF16)
    o_ref[...] += jnp.dot(act, wd_ref[...].astype(BF16), preferred_element_type=F32)

    @pl.when(j == pl.num_programs(1) - 1)
    def _():
        for rows in row_chunks:
            h = x_ref[rows, :] + 0.5 * o_ref[rows, :]
            if final_norm:
                o_ref[rows, :] = _rms_norm(h, eg_ref[...])
            else:
                o_ref[rows, :] = h
                un_ref[rows, :] = _rms_norm(h, eg_ref[...]).astype(BF16)


def _ffn(x, norm_g, wg, wu, wd, extra_g, final_norm):
    s, d = x.shape
    f = wg.shape[1]
    tm, tf = FFN_TM, FFN_TF
    row_spec = pl.BlockSpec((tm, d), lambda i, j: (i, 0))
    gain_spec = pl.BlockSpec((1, d), lambda i, j: (0, 0))
    out_specs, out_shape = row_spec, jax.ShapeDtypeStruct((s, d), F32)
    if not final_norm:
        out_specs = [row_spec, row_spec]
        out_shape = [out_shape, jax.ShapeDtypeStruct((s, d), BF16)]
    return pl.pallas_call(
        functools.partial(_ffn_kernel, final_norm=final_norm),
        grid=(s // tm, f // tf),
        in_specs=[
            row_spec,
            gain_spec,
            pl.BlockSpec((d, tf), lambda i, j: (0, j)),
            pl.BlockSpec((d, tf), lambda i, j: (0, j)),
            pl.BlockSpec((tf, d), lambda i, j: (j, 0)),
            gain_spec,
        ],
        out_specs=out_specs,
        out_shape=out_shape,
        scratch_shapes=[pltpu.VMEM((tm, d), BF16)],
        compiler_params=_compiler_params(("parallel", "arbitrary")),
        name="ffn_final" if final_norm else "ffn",
    )(x, norm_g, wg, wu, wd, extra_g)


def _gate_proj_kernel(u_ref, wt_ref, o_ref):
    o_ref[...] = lax.dot_general(u_ref[...], wt_ref[...].astype(BF16), _NT_DIMS,
                                 preferred_element_type=F32)


def _gate_proj(u, w_in_t):
    s, d = u.shape
    n = 2 * D_MODEL
    tm, tn = GATE_TM, GATE_TN
    row0 = ROW_GATES // tn
    return pl.pallas_call(
        _gate_proj_kernel,
        grid=(s // tm, n // tn),
        in_specs=[
            pl.BlockSpec((tm, d), lambda i, j: (i, 0)),
            pl.BlockSpec((tn, d), lambda i, j: (row0 + j, 0)),
        ],
        out_specs=pl.BlockSpec((tm, tn), lambda i, j: (i, j)),
        out_shape=jax.ShapeDtypeStruct((s, n), F32),
        compiler_params=_compiler_params(("parallel", "arbitrary")),
        name="gate_proj",
    )(u, w_in_t)


def _causal_conv(x, xbuf_ref, cols, cw, cb, rows):
    xbuf_ref[pl.ds(SUBLANES, rows), cols] = x
    out = cb + cw[CONV_WIDTH - 1:CONV_WIDTH, :] * x
    for back in range(1, CONV_WIDTH):
        tap = CONV_WIDTH - 1 - back
        out = out + cw[tap:tap + 1, :] * xbuf_ref[pl.ds(SUBLANES - back, rows), cols]
    xbuf_ref[pl.ds(0, SUBLANES), cols] = xbuf_ref[pl.ds(rows, SUBLANES), cols]
    return out


def _skewed_steps(n_tiles):
    return (n_tiles + 1,
            lambda s: (jnp.minimum(s, n_tiles - 1), 0),
            lambda s: (jnp.maximum(s - 1, 0), 0))


def _rg_front(n, u, wt_ref, dst_ref):
    cols = pl.ds(n * RG_BLOCK, RG_BLOCK)
    gcols = pl.ds(RG_WIDTH + n * RG_BLOCK, RG_BLOCK)
    dst_ref[0, :, cols] = lax.dot_general(u, wt_ref[cols, :], _NT_DIMS,
                                          preferred_element_type=F32)
    dst_ref[1, :, cols] = lax.dot_general(u, wt_ref[gcols, :], _NT_DIMS,
                                          preferred_element_type=F32)


def _rg_back(n, src_ref, o_ref, xbuf_ref, carry_ref, cw_ref, cb_ref, wa_ref, ba_ref, wx_ref,
             bx_ref, lam_ref):
    tm = o_ref.shape[0]
    row = lax.broadcasted_iota(jnp.int32, (SUBLANES, RG_BLOCK), 0)
    cols = pl.ds(n * RG_BLOCK, RG_BLOCK)
    xc = _causal_conv(src_ref[0, :, cols], xbuf_ref, cols, cw_ref[:, cols], cb_ref[:, cols], tm)
    xcb = xc.astype(BF16)
    r = jax.nn.sigmoid(jnp.dot(xcb, wa_ref[n], preferred_element_type=F32) + ba_ref[n])
    i = jax.nn.sigmoid(jnp.dot(xcb, wx_ref[n], preferred_element_type=F32) + bx_ref[n])
    log_a = -RG_C * r * _softplus(-lam_ref[:, cols])
    a_all = jnp.exp(log_a)
    b_all = jnp.sqrt(-jnp.tanh(log_a) * (a_all * a_all + 1.0)) * (i * xc)

    carry = carry_ref[:, cols]
    hs = []
    for r0 in range(0, tm, SUBLANES):
        a = a_all[r0:r0 + SUBLANES, :]
        b = b_all[r0:r0 + SUBLANES, :]
        for k in (1, 2, 4):
            keep = row >= k
            a_prev = pltpu.roll(a, k, axis=0)
            b_prev = pltpu.roll(b, k, axis=0)
            b = jnp.where(keep, a * b_prev + b, b)
            a = jnp.where(keep, a * a_prev, a)
        h = b + a * carry
        hs.append(h)
        carry = h[SUBLANES - 1:SUBLANES, :]
    carry_ref[:, cols] = carry
    o_ref[:, cols] = (jnp.concatenate(hs, axis=0)
                      * jax.nn.gelu(src_ref[1, :, cols])).astype(o_ref.dtype)


def _rg_mixer_kernel(u_ref, wt_ref, cw_ref, cb_ref, wa_ref, ba_ref, wx_ref, bx_ref, lam_ref,
                     o_ref, p0_ref, p1_ref, xbuf_ref, carry_ref):
    s = pl.program_id(0)

    @pl.when(s == 0)
    def _():
        p1_ref[...] = jnp.zeros_like(p1_ref)

    @pl.when(s <= 1)
    def _():
        xbuf_ref[pl.ds(0, SUBLANES), :] = jnp.zeros((SUBLANES, RG_WIDTH), F32)
        carry_ref[...] = jnp.zeros_like(carry_ref)

    def step(dst_ref, src_ref):
        u = u_ref[...]
        for n in range(RG_BLOCKS):
            _rg_front(n, u, wt_ref, dst_ref)
            _rg_back(n, src_ref, o_ref, xbuf_ref, carry_ref, cw_ref, cb_ref, wa_ref, ba_ref,
                     wx_ref, bx_ref, lam_ref)

    pl.when(s % 2 == 0)(lambda: step(p0_ref, p1_ref))
    pl.when(s % 2 == 1)(lambda: step(p1_ref, p0_ref))


def _rg_mixer(u, w_t, conv_w, conv_b, w_a, b_a, w_x, b_x, lam):
    s, d = u.shape
    tm = MIX_TM
    steps, cur_map, prev_map = _skewed_steps(s // tm)
    blk = RG_BLOCK
    return pl.pallas_call(
        _rg_mixer_kernel,
        grid=(steps,),
        in_specs=[
            pl.BlockSpec((tm, d), cur_map),
            _resident((2 * RG_WIDTH, d)),
            _resident((CONV_WIDTH, RG_WIDTH)),
            _resident((1, RG_WIDTH)),
            _resident((RG_BLOCKS, blk, blk)),
            _resident((RG_BLOCKS, 1, blk)),
            _resident((RG_BLOCKS, blk, blk)),
            _resident((RG_BLOCKS, 1, blk)),
            _resident((1, RG_WIDTH)),
        ],
        out_specs=pl.BlockSpec((tm, RG_WIDTH), prev_map),
        out_shape=jax.ShapeDtypeStruct((s, RG_WIDTH), BF16),
        scratch_shapes=[
            pltpu.VMEM((2, tm, RG_WIDTH), F32),
            pltpu.VMEM((2, tm, RG_WIDTH), F32),
            pltpu.VMEM((tm + SUBLANES, RG_WIDTH), F32),
            pltpu.VMEM((1, RG_WIDTH), F32),
        ],
        compiler_params=_compiler_params(("arbitrary",)),
        name="rg_mixer",
    )(u, w_t, conv_w, conv_b, w_a, b_a, w_x, b_x, lam)


def _mlstm_chunk(ux, mo, gif, o_ref, rows, xbuf_ref, c_ref, n_ref, m_ref, cw_ref, cb_ref,
                 wq_ref, wk_ref, wv_ref, bif_ref, ng_ref, skip_ref):
    L, D = M_CHUNK, M_HEAD_DIM
    conv = _causal_conv(ux, xbuf_ref, slice(None), cw_ref[...], cb_ref[...], L)
    xc = conv * jax.nn.sigmoid(conv)

    gt = (gif + bif_ref[...]).T
    log_i = gt[0:M_HEADS, :]
    log_f = -_softplus(-gt[M_HEADS:2 * M_HEADS, :])
    lane = lax.broadcasted_iota(jnp.int32, (M_HEADS, L), 1)
    b = log_f
    k = 1
    while k < L:
        b = b + jnp.where(lane >= k, pltpu.roll(b, k, axis=1), 0.0)
        k *= 2
    g = b[:, L - 1:L]
    r = log_i - b
    a = g + r
    m_loc = jnp.max(a, axis=1, keepdims=True)
    w = jnp.exp(a - m_loc)
    m_prev = m_ref[:, 0:1]
    m_new = jnp.maximum(g + m_prev, m_loc)
    s_old = jnp.exp(g + m_prev - m_new)
    s_loc = jnp.exp(m_loc - m_new)
    m_ref[...] = jnp.broadcast_to(m_new, m_ref.shape)

    rmax = r
    k = 1
    while k < L:
        rmax = jnp.maximum(rmax, jnp.where(lane >= k, pltpu.roll(rmax, k, axis=1), -jnp.inf))
        k *= 2
    cm = jnp.maximum(m_prev, rmax)
    iw = jnp.exp(m_prev - cm)
    em = jnp.exp(-(b + cm))

    cols = jnp.concatenate([cm, w, iw, em, jnp.zeros((LANES - 4 * M_HEADS, L), F32)], axis=0).T

    tri = (lax.broadcasted_iota(jnp.int32, (L, L), 0)
           >= lax.broadcasted_iota(jnp.int32, (L, L), 1))

    heads = range(M_HEADS)
    sls = [slice(h * D, (h + 1) * D) for h in heads]

    xcb = [xc[:, sl].astype(BF16) for sl in sls]
    q = [jnp.dot(xcb[h], wq_ref[h], preferred_element_type=F32) for h in heads]
    kk = [jnp.dot(xcb[h], wk_ref[h], preferred_element_type=F32) * (M_HEAD_DIM ** -0.5)
          for h in heads]
    v = [jnp.dot(ux[:, sls[h]].astype(BF16), wv_ref[h], preferred_element_type=F32)
         for h in heads]
    qb = [x.astype(BF16) for x in q]
    kb = [x.astype(BF16) for x in kk]
    qk = [lax.dot_general(qb[h], kb[h], _NT_DIMS, preferred_element_type=F32) for h in heads]

    col = lambda which, h: cols[:, which * M_HEADS + h:which * M_HEADS + h + 1]
    cm_col = [col(0, h) for h in heads]
    w_col = [col(1, h) for h in heads]
    inter_w = [col(2, h) for h in heads]
    em_col = [col(3, h) for h in heads]
    c_prev = [c_ref[h] for h in heads]
    n_prev = [n_ref[h] for h in heads]
    qc = [jnp.dot(qb[h], c_prev[h].astype(BF16), preferred_element_type=F32) for h in heads]
    c_loc = [jnp.dot(kk[h].T.astype(BF16), (w_col[h] * v[h]).astype(BF16),
                     preferred_element_type=F32) for h in heads]

    scores = [qk[h] * jnp.exp(jnp.where(tri, r[h:h + 1, :] - cm_col[h], -jnp.inf))
              for h in heads]
    sv = [jnp.dot(scores[h].astype(BF16), v[h].astype(BF16), preferred_element_type=F32)
          for h in heads]

    for h in heads:
        s_old_h = s_old[h:h + 1, :]
        s_loc_h = s_loc[h:h + 1, :]
        c_ref[h] = s_old_h * c_prev[h] + s_loc_h * c_loc[h]
        n_loc = jnp.sum(w_col[h] * kk[h], axis=0, keepdims=True)
        n_ref[h] = s_old_h * n_prev[h] + s_loc_h * n_loc

    for h in heads:
        sl = sls[h]
        num = sv[h] + inter_w[h] * qc[h]
        den = (jnp.sum(scores[h], axis=1, keepdims=True)
               + inter_w[h] * jnp.sum(q[h] * n_prev[h], axis=1, keepdims=True))
        hh = num * (1.0 / jnp.maximum(jnp.abs(den), em_col[h]))
        mu = jnp.mean(hh, axis=1, keepdims=True)
        cen = hh - mu
        var = jnp.mean(cen * cen, axis=1, keepdims=True)
        y = cen * lax.rsqrt(var + NORM_EPS) * ng_ref[:, sl] + skip_ref[:, sl] * xc[:, sl]
        o_ref[rows, sl] = (jax.nn.sigmoid(mo[:, sl]) * y).astype(o_ref.dtype)


def _m_mixer_kernel(u_ref, wt_ref, wift_ref, cw_ref, cb_ref, wq_ref, wk_ref, wv_ref, bif_ref,
                    ng_ref, skip_ref, o_ref, p0_ref, p1_ref, g0_ref, g1_ref, xbuf_ref, c_ref,
                    n_ref, m_ref):
    s = pl.program_id(0)

    @pl.when(s == 0)
    def _():
        p1_ref[...] = jnp.zeros_like(p1_ref)
        g1_ref[...] = jnp.zeros_like(g1_ref)

    @pl.when(s <= 1)
    def _():
        xbuf_ref[pl.ds(0, SUBLANES), :] = jnp.zeros((SUBLANES, M_WIDTH), F32)
        c_ref[...] = jnp.zeros_like(c_ref)
        n_ref[...] = jnp.zeros_like(n_ref)
        m_ref[...] = jnp.zeros_like(m_ref)

    def step(dst_ref, gdst_ref, src_ref, gsrc_ref):
        u = u_ref[...]
        for c0 in range(0, 2 * M_WIDTH, M_PROJ_PIECE):
            cols = pl.ds(c0, M_PROJ_PIECE)
            dst_ref[:, cols] = lax.dot_general(u, wt_ref[cols, :], _NT_DIMS,
                                               preferred_element_type=F32)
        gdst_ref[...] = lax.dot_general(u, wift_ref[...], _NT_DIMS, preferred_element_type=F32)
        for r0 in range(0, o_ref.shape[0], M_CHUNK):
            rows = pl.ds(r0, M_CHUNK)
            _mlstm_chunk(src_ref[rows, pl.ds(0, M_WIDTH)], src_ref[rows, pl.ds(M_WIDTH, M_WIDTH)],
                         gsrc_ref[rows, :], o_ref, rows, xbuf_ref, c_ref, n_ref, m_ref, cw_ref,
                         cb_ref, wq_ref, wk_ref, wv_ref, bif_ref, ng_ref, skip_ref)

    pl.when(s % 2 == 0)(lambda: step(p0_ref, g0_ref, p1_ref, g1_ref))
    pl.when(s % 2 == 1)(lambda: step(p1_ref, g1_ref, p0_ref, g0_ref))


def _m_mixer(u, w_t, w_if_t, conv_w, conv_b, w_q, w_k, w_v, b_if, norm_g, skip):
    s, d = u.shape
    tm = MIX_TM
    steps, cur_map, prev_map = _skewed_steps(s // tm)
    head_w = (M_HEADS, M_HEAD_DIM, M_HEAD_DIM)
    return pl.pallas_call(
        _m_mixer_kernel,
        grid=(steps,),
        in_specs=[
            pl.BlockSpec((tm, d), cur_map),
            _resident((2 * M_WIDTH, d)),
            _resident((LANES, d)),
            _resident((CONV_WIDTH, M_WIDTH)),
            _resident((1, M_WIDTH)),
            _resident(head_w),
            _resident(head_w),
            _resident(head_w),
            _resident((1, LANES)),
            _resident((1, M_WIDTH)),
            _resident((1, M_WIDTH)),
        ],
        out_specs=pl.BlockSpec((tm, M_WIDTH), prev_map),
        out_shape=jax.ShapeDtypeStruct((s, M_WIDTH), BF16),
        scratch_shapes=[
            pltpu.VMEM((tm, 2 * M_WIDTH), F32),
            pltpu.VMEM((tm, 2 * M_WIDTH), F32),
            pltpu.VMEM((tm, LANES), F32),
            pltpu.VMEM((tm, LANES), F32),
            pltpu.VMEM((M_CHUNK + SUBLANES, M_WIDTH), F32),
            pltpu.VMEM(head_w, F32),
            pltpu.VMEM((M_HEADS, 1, M_HEAD_DIM), F32),
            pltpu.VMEM((M_HEADS, LANES), F32),
        ],
        compiler_params=_compiler_params(("arbitrary",)),
        name="m_mixer",
    )(u, w_t, w_if_t, conv_w, conv_b, w_q, w_k, w_v, b_if, norm_g, skip)


def _merge_kernel(yrg_ref, ym_ref, grg_ref, gm_ref, wrg_ref, wm_ref, wout_ref, h_ref, o_ref):
    merged = (jax.nn.sigmoid(grg_ref[...])
              * jnp.dot(yrg_ref[...], wrg_ref[...], preferred_element_type=F32)
              + jax.nn.sigmoid(gm_ref[...])
              * jnp.dot(ym_ref[...], wm_ref[...], preferred_element_type=F32))
    o_ref[...] = h_ref[...] + jnp.dot(merged.astype(BF16), wout_ref[...],
                                      preferred_element_type=F32)


def _merge(y_rg, y_m, gates, w_rg, w_m, w_out, h1):
    s, d = h1.shape
    tm = MERGE_TM
    return pl.pallas_call(
        _merge_kernel,
        grid=(s // tm,),
        in_specs=[
            pl.BlockSpec((tm, RG_WIDTH), lambda i: (i, 0)),
            pl.BlockSpec((tm, M_WIDTH), lambda i: (i, 0)),
            pl.BlockSpec((tm, d), lambda i: (i, 0)),
            pl.BlockSpec((tm, d), lambda i: (i, 1)),
            _resident((RG_WIDTH, d)),
            _resident((M_WIDTH, d)),
            _resident((d, d)),
            pl.BlockSpec((tm, d), lambda i: (i, 0)),
        ],
        out_specs=pl.BlockSpec((tm, d), lambda i: (i, 0)),
        out_shape=jax.ShapeDtypeStruct((s, d), F32),
        compiler_params=_compiler_params(("parallel",)),
        name="merge",
    )(y_rg, y_m, gates, gates, w_rg, w_m, w_out, h1)


def kernel(x, ffn1_norm, ffn1_w_gate, ffn1_w_up, ffn1_w_down, mix_norm, w_in, rg_conv_w, rg_conv_b, rg_w_a, rg_b_a, rg_w_x, rg_b_x, rg_lambda, m_conv_w, m_conv_b, m_w_q, m_w_k, m_w_v, m_b_i, m_b_f, m_norm, m_skip, w_proj_rg, w_proj_m, w_out, ffn2_norm, ffn2_w_gate, ffn2_w_up, ffn2_w_down, final_norm):
    bsz, s, d = x.shape
    assert (bsz, s, d) == (1, SEQ, D_MODEL) and ffn1_norm.shape[0] == 1
    bf = lambda w: w.astype(BF16)
    h = x[0]

    h, u_mix = _ffn(h, ffn1_norm, ffn1_w_gate[0], ffn1_w_up[0], ffn1_w_down[0], mix_norm,
                    final_norm=False)

    w_in_t = jnp.swapaxes(w_in[0], 0, 1)
    w_if_t = jnp.pad(w_in_t[ROW_IF:], ((0, LANES - 2 * M_HEADS), (0, 0)))
    gates = _gate_proj(u_mix, w_in_t)

    y_rg = _rg_mixer(u_mix, bf(w_in_t[ROW_RG:ROW_M]), rg_conv_w[0], rg_conv_b, bf(rg_w_a[0]),
                     rg_b_a[0][:, None, :], bf(rg_w_x[0]), rg_b_x[0][:, None, :], rg_lambda)

    b_if = jnp.pad(jnp.concatenate([m_b_i, m_b_f], axis=1), ((0, 0), (0, LANES - 2 * M_HEADS)))
    y_m = _m_mixer(u_mix, bf(w_in_t[ROW_M:ROW_GATES]), bf(w_if_t), m_conv_w[0], m_conv_b,
                   bf(m_w_q[0]), bf(m_w_k[0]), bf(m_w_v[0]), b_if, m_norm, m_skip)

    h = _merge(y_rg, y_m, gates, bf(w_proj_rg[0]), bf(w_proj_m[0]), bf(w_out[0]), h)

    h = _ffn(h, ffn2_norm, ffn2_w_gate[0], ffn2_w_up[0], ffn2_w_down[0], final_norm[None, :],
             final_norm=True)
    return h[None]
```

```python
import functools

import jax
import jax.numpy as jnp
from jax import lax
from jax.experimental import pallas as pl
from jax.experimental.pallas import tpu as pltpu

D_MODEL = 2048
SEQ = 8192
D_FF = 5632
CONV_WIDTH = 4
RG_WIDTH = 2048
RG_BLOCKS = 8
RG_BLOCK = RG_WIDTH // RG_BLOCKS
RG_C = 8.0
M_HEADS = 8
M_HEAD_DIM = 256
M_WIDTH = M_HEADS * M_HEAD_DIM
M_CHUNK = 128
NORM_EPS = 1e-6

LANES = 128
SUBLANES = 8
V7X_VMEM_BYTES = 64 * 1024 * 1024
VMEM_LIMIT_BYTES = (V7X_VMEM_BYTES // 16) * 15

ROW_RG = 0
ROW_M = 2 * RG_WIDTH
ROW_GATES = ROW_M + 2 * M_WIDTH
ROW_IF = ROW_GATES + 2 * D_MODEL

FFN_TM, FFN_TF = 1024, 256
FFN_ROW_CHUNK = 128
GATE_TM, GATE_TN = 2048, 512
MIX_TM = 256
M_PROJ_PIECE = 512
CAST_ROWS = 512
MERGE_TM = 256

BF16 = jnp.bfloat16
F32 = jnp.float32
_NT_DIMS = (((1,), (1,)), ((), ()))


def _rms_norm(x, g):
    ms = jnp.mean(x * x, axis=-1, keepdims=True)
    return x * lax.rsqrt(ms + NORM_EPS) * g


def _softplus(z):
    return jnp.maximum(z, 0.0) + jnp.log1p(jnp.exp(-jnp.abs(z)))


def _compiler_params(semantics):
    return pltpu.CompilerParams(dimension_semantics=semantics,
                                vmem_limit_bytes=VMEM_LIMIT_BYTES)


def _resident(shape, row_block=0):
    index = (row_block,) + (0,) * (len(shape) - 1)
    return pl.BlockSpec(shape, lambda *_: index, pipeline_mode=pl.Buffered(1))


def _cast_kernel(x_ref, o_ref):
    o_ref[...] = x_ref[...].astype(BF16)


def _cast_rows(w, n_rows):
    d = w.shape[1]
    spec = pl.BlockSpec((CAST_ROWS, d), lambda i: (i, 0))
    return pl.pallas_call(
        _cast_kernel,
        grid=(n_rows // CAST_ROWS,),
        in_specs=[spec],
        out_specs=spec,
        out_shape=jax.ShapeDtypeStruct((n_rows, d), BF16),
        compiler_params=_compiler_params(("parallel",)),
        name="cast_rows",
    )(w)


def _ffn_kernel(*refs, final_norm):
    if final_norm:
        x_ref, g_ref, wg_ref, wu_ref, wd_ref, eg_ref, o_ref, u_ref = refs
    else:
        x_ref, g_ref, wg_ref, wu_ref, wd_ref, eg_ref, o_ref, un_ref, u_ref = refs
    j = pl.program_id(1)

    row_chunks = [pl.ds(r, FFN_ROW_CHUNK) for r in range(0, x_ref.shape[0], FFN_ROW_CHUNK)]

    @pl.when(j == 0)
    def _():
        for rows in row_chunks:
            u_ref[rows, :] = _rms_norm(x_ref[rows, :], g_ref[...]).astype(BF16)
        o_ref[...] = jnp.zeros_like(o_ref)

    u = u_ref[...]
    gate = jnp.dot(u, wg_ref[...].astype(BF16), preferred_element_type=F32)
    up = jnp.dot(u, wu_ref[...].astype(BF16), preferred_element_type=F32)
    act = (gate * jax.nn.sigmoid(gate) * up).astype(BF16)
    o_ref[...] += jnp.dot(act, wd_ref[...].astype(BF16), preferred_element_type=F32)

    @pl.when(j == pl.num_programs(1) - 1)
    def _():
        for rows in row_chunks:
            h = x_ref[rows, :] + 0.5 * o_ref[rows, :]
            if final_norm:
                o_ref[rows, :] = _rms_norm(h, eg_ref[...])
            else:
                o_ref[rows, :] = h
                un_ref[rows, :] = _rms_norm(h, eg_ref[...]).astype(BF16)


def _ffn(x, norm_g, wg, wu, wd, extra_g, final_norm):
    s, d = x.shape
    f = wg.shape[1]
    tm, tf = FFN_TM, FFN_TF
    row_spec = pl.BlockSpec((tm, d), lambda i, j: (i, 0))
    gain_spec = pl.BlockSpec((1, d), lambda i, j: (0, 0))
    out_specs, out_shape = row_spec, jax.ShapeDtypeStruct((s, d), F32)
    if not final_norm:
        out_specs = [row_spec, row_spec]
        out_shape = [out_shape, jax.ShapeDtypeStruct((s, d), BF16)]
    return pl.pallas_call(
        functools.partial(_ffn_kernel, final_norm=final_norm),
        grid=(s // tm, f // tf),
        in_specs=[
            row_spec,
            gain_spec,
            pl.BlockSpec((d, tf), lambda i, j: (0, j)),
            pl.BlockSpec((d, tf), lambda i, j: (0, j)),
            pl.BlockSpec((tf, d), lambda i, j: (j, 0)),
            gain_spec,
        ],
        out_specs=out_specs,
        out_shape=out_shape,
        scratch_shapes=[pltpu.VMEM((tm, d), BF16)],
        compiler_params=_compiler_params(("parallel", "arbitrary")),
        name="ffn_final" if final_norm else "ffn",
    )(x, norm_g, wg, wu, wd, extra_g)


def _gate_proj_kernel(u_ref, wt_ref, o_ref):
    o_ref[...] = lax.dot_general(u_ref[...], wt_ref[...].astype(BF16), _NT_DIMS,
                                 preferred_element_type=F32)


def _gate_proj(u, w_in_t):
    s, d = u.shape
    n = 2 * D_MODEL
    tm, tn = GATE_TM, GATE_TN
    row0 = ROW_GATES // tn
    return pl.pallas_call(
        _gate_proj_kernel,
        grid=(s // tm, n // tn),
        in_specs=[
            pl.BlockSpec((tm, d), lambda i, j: (i, 0)),
            pl.BlockSpec((tn, d), lambda i, j: (row0 + j, 0)),
        ],
        out_specs=pl.BlockSpec((tm, tn), lambda i, j: (i, j)),
        out_shape=jax.ShapeDtypeStruct((s, n), F32),
        compiler_params=_compiler_params(("parallel", "arbitrary")),
        name="gate_proj",
    )(u, w_in_t)


def _causal_conv(x, xbuf_ref, cols, cw, cb, rows):
    xbuf_ref[pl.ds(SUBLANES, rows), cols] = x
    out = cb + cw[CONV_WIDTH - 1:CONV_WIDTH, :] * x
    for back in range(1, CONV_WIDTH):
        tap = CONV_WIDTH - 1 - back
        out = out + cw[tap:tap + 1, :] * xbuf_ref[pl.ds(SUBLANES - back, rows), cols]
    xbuf_ref[pl.ds(0, SUBLANES), cols] = xbuf_ref[pl.ds(rows, SUBLANES), cols]
    return out


def _skewed_steps(n_tiles):
    return (n_tiles + 1,
            lambda s: (jnp.minimum(s, n_tiles - 1), 0),
            lambda s: (jnp.maximum(s - 1, 0), 0))


def _rg_front(n, u, wt_ref, dst_ref):
    rows_x = pl.ds(n * RG_BLOCK, RG_BLOCK)
    rows_g = pl.ds(RG_WIDTH + n * RG_BLOCK, RG_BLOCK)
    dst_ref[0, n] = lax.dot_general(u, wt_ref[rows_x, :], _NT_DIMS, preferred_element_type=F32)
    dst_ref[1, n] = lax.dot_general(u, wt_ref[rows_g, :], _NT_DIMS, preferred_element_type=F32)


def _rg_back(n, src_ref, obuf_ref, xbuf_ref, carry_ref, cw_ref, cb_ref, wa_ref, ba_ref, wx_ref,
             bx_ref, lam_ref):
    tm = obuf_ref.shape[1]
    row = lax.broadcasted_iota(jnp.int32, (SUBLANES, RG_BLOCK), 0)
    xc = _causal_conv(src_ref[0, n], xbuf_ref.at[n], slice(None), cw_ref[n], cb_ref[n], tm)
    xcb = xc.astype(BF16)
    r = jax.nn.sigmoid(jnp.dot(xcb, wa_ref[n], preferred_element_type=F32) + ba_ref[n])
    i = jax.nn.sigmoid(jnp.dot(xcb, wx_ref[n], preferred_element_type=F32) + bx_ref[n])
    log_a = -RG_C * r * _softplus(-lam_ref[n])
    a_all = jnp.exp(log_a)
    v = -jnp.tanh(log_a) * (a_all * a_all + 1.0)
    b_all = jnp.where(v > 0.0, v * lax.rsqrt(v), 0.0) * (i * xc)

    carry = carry_ref[n]
    hs = []
    for r0 in range(0, tm, SUBLANES):
        a = a_all[r0:r0 + SUBLANES, :]
        b = b_all[r0:r0 + SUBLANES, :]
        for k in (1, 2, 4):
            keep = row >= k
            a_prev = pltpu.roll(a, k, axis=0)
            b_prev = pltpu.roll(b, k, axis=0)
            b = jnp.where(keep, a * b_prev + b, b)
            a = jnp.where(keep, a * a_prev, a)
        h = b + a * carry
        hs.append(h)
        carry = h[SUBLANES - 1:SUBLANES, :]
    carry_ref[n] = carry
    obuf_ref[n] = (jnp.concatenate(hs, axis=0) * jax.nn.gelu(src_ref[1, n])).astype(BF16)


def _rg_mixer_kernel(u_ref, wt_ref, cw_ref, cb_ref, wa_ref, ba_ref, wx_ref, bx_ref, lam_ref,
                     o_ref, p0_ref, p1_ref, xbuf_ref, carry_ref, obuf_ref):
    s = pl.program_id(0)

    @pl.when(s == 0)
    def _():
        p1_ref[...] = jnp.zeros_like(p1_ref)

    @pl.when(s <= 1)
    def _():
        xbuf_ref[:, pl.ds(0, SUBLANES), :] = jnp.zeros((RG_BLOCKS, SUBLANES, RG_BLOCK), F32)
        carry_ref[...] = jnp.zeros_like(carry_ref)

    def step(dst_ref, src_ref):
        u = u_ref[...]
        for n in range(RG_BLOCKS):
            _rg_front(n, u, wt_ref, dst_ref)
            _rg_back(n, src_ref, obuf_ref, xbuf_ref, carry_ref, cw_ref, cb_ref, wa_ref, ba_ref,
                     wx_ref, bx_ref, lam_ref)
            o_ref[:, pl.ds(n * RG_BLOCK, RG_BLOCK)] = obuf_ref[n]

    pl.when(s % 2 == 0)(lambda: step(p0_ref, p1_ref))
    pl.when(s % 2 == 1)(lambda: step(p1_ref, p0_ref))


def _rg_mixer(u, w_t, conv_w, conv_b, w_a, b_a, w_x, b_x, lam):
    s, d = u.shape
    tm = MIX_TM
    steps, cur_map, prev_map = _skewed_steps(s // tm)
    blk = RG_BLOCK
    return pl.pallas_call(
        _rg_mixer_kernel,
        grid=(steps,),
        in_specs=[
            pl.BlockSpec((tm, d), cur_map),
            _resident((2 * RG_WIDTH, d), ROW_RG // (2 * RG_WIDTH)),
            _resident((RG_BLOCKS, CONV_WIDTH, blk)),
            _resident((RG_BLOCKS, 1, blk)),
            _resident((RG_BLOCKS, blk, blk)),
            _resident((RG_BLOCKS, 1, blk)),
            _resident((RG_BLOCKS, blk, blk)),
            _resident((RG_BLOCKS, 1, blk)),
            _resident((RG_BLOCKS, 1, blk)),
        ],
        out_specs=pl.BlockSpec((tm, RG_WIDTH), prev_map),
        out_shape=jax.ShapeDtypeStruct((s, RG_WIDTH), BF16),
        scratch_shapes=[
            pltpu.VMEM((2, RG_BLOCKS, tm, blk), F32),
            pltpu.VMEM((2, RG_BLOCKS, tm, blk), F32),
            pltpu.VMEM((RG_BLOCKS, tm + SUBLANES, blk), F32),
            pltpu.VMEM((RG_BLOCKS, 1, blk), F32),
            pltpu.VMEM((RG_BLOCKS, tm, blk), BF16),
        ],
        compiler_params=_compiler_params(("arbitrary",)),
        name="rg_mixer",
    )(u, w_t, conv_w, conv_b, w_a, b_a, w_x, b_x, lam)


def _mlstm_chunk(ux, mo, gif, o_ref, rows, xbuf_ref, c_ref, n_ref, m_ref, cw_ref, cb_ref,
                 wq_ref, wk_ref, wv_ref, bif_ref, ng_ref, skip_ref):
    L, D = M_CHUNK, M_HEAD_DIM
    conv = _causal_conv(ux, xbuf_ref, slice(None), cw_ref[...], cb_ref[...], L)
    xc = conv * jax.nn.sigmoid(conv)

    gt = (gif + bif_ref[...]).T
    log_i = gt[0:M_HEADS, :]
    log_f = -_softplus(-gt[M_HEADS:2 * M_HEADS, :])
    lane = lax.broadcasted_iota(jnp.int32, (M_HEADS, L), 1)
    b = log_f
    k = 1
    while k < L:
        b = b + jnp.where(lane >= k, pltpu.roll(b, k, axis=1), 0.0)
        k *= 2
    g = b[:, L - 1:L]
    r = log_i - b
    a = g + r
    m_loc = jnp.max(a, axis=1, keepdims=True)
    w = jnp.exp(a - m_loc)
    m_prev = m_ref[:, 0:1]
    m_new = jnp.maximum(g + m_prev, m_loc)
    s_old = jnp.exp(g + m_prev - m_new)
    s_loc = jnp.exp(m_loc - m_new)
    m_ref[...] = jnp.broadcast_to(m_new, m_ref.shape)

    rmax = r
    k = 1
    while k < L:
        rmax = jnp.maximum(rmax, jnp.where(lane >= k, pltpu.roll(rmax, k, axis=1), -jnp.inf))
        k *= 2
    cm = jnp.maximum(m_prev, rmax)
    iw = jnp.exp(m_prev - cm)
    em = jnp.exp(-(b + cm))

    cols = jnp.concatenate([cm, w, iw, em, jnp.zeros((LANES - 4 * M_HEADS, L), F32)], axis=0).T

    tri = (lax.broadcasted_iota(jnp.int32, (L, L), 0)
           >= lax.broadcasted_iota(jnp.int32, (L, L), 1))

    heads = range(M_HEADS)
    sls = [slice(h * D, (h + 1) * D) for h in heads]

    xcb = [xc[:, sl].astype(BF16) for sl in sls]
    q = [jnp.dot(xcb[h], wq_ref[h], preferred_element_type=F32) for h in heads]
    kk = [jnp.dot(xcb[h], wk_ref[h], preferred_element_type=F32) * (M_HEAD_DIM ** -0.5)
          for h in heads]
    v = [jnp.dot(ux[:, sls[h]].astype(BF16), wv_ref[h], preferred_element_type=F32)
         for h in heads]
    qb = [x.astype(BF16) for x in q]
    kb = [x.astype(BF16) for x in kk]
    qk = [lax.dot_general(qb[h], kb[h], _NT_DIMS, preferred_element_type=F32) for h in heads]

    col = lambda which, h: cols[:, which * M_HEADS + h:which * M_HEADS + h + 1]
    cm_col = [col(0, h) for h in heads]
    w_col = [col(1, h) for h in heads]
    inter_w = [col(2, h) for h in heads]
    em_col = [col(3, h) for h in heads]
    c_prev = [c_ref[h] for h in heads]
    n_prev = [n_ref[h] for h in heads]
    qc = [jnp.dot(qb[h], c_prev[h].astype(BF16), preferred_element_type=F32) for h in heads]
    c_loc = [jnp.dot(kk[h].T.astype(BF16), (w_col[h] * v[h]).astype(BF16),
                     preferred_element_type=F32) for h in heads]

    scores = [qk[h] * jnp.exp(jnp.where(tri, r[h:h + 1, :] - cm_col[h], -jnp.inf))
              for h in heads]
    sv = [jnp.dot(scores[h].astype(BF16), v[h].astype(BF16), preferred_element_type=F32)
          for h in heads]

    for h in heads:
        s_old_h = s_old[h:h + 1, :]
        s_loc_h = s_loc[h:h + 1, :]
        c_ref[h] = s_old_h * c_prev[h] + s_loc_h * c_loc[h]
        n_loc = jnp.sum(w_col[h] * kk[h], axis=0, keepdims=True)
        n_ref[h] = s_old_h * n_prev[h] + s_loc_h * n_loc

    for h in heads:
        sl = sls[h]
        num = sv[h] + inter_w[h] * qc[h]
        den = (jnp.sum(scores[h], axis=1, keepdims=True)
               + inter_w[h] * jnp.sum(q[h] * n_prev[h], axis=1, keepdims=True))
        hh = num * (1.0 / jnp.maximum(jnp.abs(den), em_col[h]))
        mu = jnp.mean(hh, axis=1, keepdims=True)
        cen = hh - mu
        var = jnp.mean(cen * cen, axis=1, keepdims=True)
        y = cen * lax.rsqrt(var + NORM_EPS) * ng_ref[:, sl] + skip_ref[:, sl] * xc[:, sl]
        o_ref[rows, sl] = (jax.nn.sigmoid(mo[:, sl]) * y).astype(o_ref.dtype)


def _m_mixer_kernel(u_ref, wt_ref, wift_ref, cw_ref, cb_ref, wq_ref, wk_ref, wv_ref, bif_ref,
                    ng_ref, skip_ref, o_ref, p0_ref, p1_ref, g0_ref, g1_ref, xbuf_ref, c_ref,
                    n_ref, m_ref):
    s = pl.program_id(0)

    @pl.when(s == 0)
    def _():
        p1_ref[...] = jnp.zeros_like(p1_ref)
        g1_ref[...] = jnp.zeros_like(g1_ref)

    @pl.when(s <= 1)
    def _():
        xbuf_ref[pl.ds(0, SUBLANES), :] = jnp.zeros((SUBLANES, M_WIDTH), F32)
        c_ref[...] = jnp.zeros_like(c_ref)
        n_ref[...] = jnp.zeros_like(n_ref)
        m_ref[...] = jnp.zeros_like(m_ref)

    def step(dst_ref, gdst_ref, src_ref, gsrc_ref):
        u = u_ref[...]
        for c0 in range(0, 2 * M_WIDTH, M_PROJ_PIECE):
            cols = pl.ds(c0, M_PROJ_PIECE)
            dst_ref[:, cols] = lax.dot_general(u, wt_ref[cols, :], _NT_DIMS,
                                               preferred_element_type=F32)
        gdst_ref[...] = lax.dot_general(u, wift_ref[...], _NT_DIMS, preferred_element_type=F32)

        for r0 in range(0, o_ref.shape[0], M_CHUNK):
            rows = pl.ds(r0, M_CHUNK)
            _mlstm_chunk(src_ref[rows, pl.ds(0, M_WIDTH)], src_ref[rows, pl.ds(M_WIDTH, M_WIDTH)],
                         gsrc_ref[rows, :], o_ref, rows, xbuf_ref, c_ref, n_ref, m_ref, cw_ref,
                         cb_ref, wq_ref, wk_ref, wv_ref, bif_ref, ng_ref, skip_ref)

    pl.when(s % 2 == 0)(lambda: step(p0_ref, g0_ref, p1_ref, g1_ref))
    pl.when(s % 2 == 1)(lambda: step(p1_ref, g1_ref, p0_ref, g0_ref))


def _m_mixer(u, w_t, w_if_t, conv_w, conv_b, w_q, w_k, w_v, b_if, norm_g, skip):
    s, d = u.shape
    tm = MIX_TM
    steps, cur_map, prev_map = _skewed_steps(s // tm)
    head_w = (M_HEADS, M_HEAD_DIM, M_HEAD_DIM)
    return pl.pallas_call(
        _m_mixer_kernel,
        grid=(steps,),
        in_specs=[
            pl.BlockSpec((tm, d), cur_map),
            _resident((2 * M_WIDTH, d), ROW_M // (2 * M_WIDTH)),
            _resident((LANES, d)),
            _resident((CONV_WIDTH, M_WIDTH)),
            _resident((1, M_WIDTH)),
            _resident(head_w),
            _resident(head_w),
            _resident(head_w),
            _resident((1, LANES)),
            _resident((1, M_WIDTH)),
            _resident((1, M_WIDTH)),
        ],
        out_specs=pl.BlockSpec((tm, M_WIDTH), prev_map),
        out_shape=jax.ShapeDtypeStruct((s, M_WIDTH), BF16),
        scratch_shapes=[
            pltpu.VMEM((tm, 2 * M_WIDTH), F32),
            pltpu.VMEM((tm, 2 * M_WIDTH), F32),
            pltpu.VMEM((tm, LANES), F32),
            pltpu.VMEM((tm, LANES), F32),
            pltpu.VMEM((M_CHUNK + SUBLANES, M_WIDTH), F32),
            pltpu.VMEM(head_w, F32),
            pltpu.VMEM((M_HEADS, 1, M_HEAD_DIM), F32),
            pltpu.VMEM((M_HEADS, LANES), F32),
        ],
        compiler_params=_compiler_params(("arbitrary",)),
        name="m_mixer",
    )(u, w_t, w_if_t, conv_w, conv_b, w_q, w_k, w_v, b_if, norm_g, skip)


def _merge_kernel(yrg_ref, ym_ref, grg_ref, gm_ref, wrg_ref, wm_ref, wout_ref, h_ref, o_ref):
    merged = (jax.nn.sigmoid(grg_ref[...])
              * jnp.dot(yrg_ref[...], wrg_ref[...], preferred_element_type=F32)
              + jax.nn.sigmoid(gm_ref[...])
              * jnp.dot(ym_ref[...], wm_ref[...], preferred_element_type=F32))
    o_ref[...] = h_ref[...] + jnp.dot(merged.astype(BF16), wout_ref[...],
                                      preferred_element_type=F32)


def _merge(y_rg, y_m, gates, w_rg, w_m, w_out, h1):
    s, d = h1.shape
    tm = MERGE_TM
    return pl.pallas_call(
        _merge_kernel,
        grid=(s // tm,),
        in_specs=[
            pl.BlockSpec((tm, RG_WIDTH), lambda i: (i, 0)),
            pl.BlockSpec((tm, M_WIDTH), lambda i: (i, 0)),
            pl.BlockSpec((tm, d), lambda i: (i, 0)),
            pl.BlockSpec((tm, d), lambda i: (i, 1)),
            _resident((RG_WIDTH, d)),
            _resident((M_WIDTH, d)),
            _resident((d, d)),
            pl.BlockSpec((tm, d), lambda i: (i, 0)),
        ],
        out_specs=pl.BlockSpec((tm, d), lambda i: (i, 0)),
        out_shape=jax.ShapeDtypeStruct((s, d), F32),
        compiler_params=_compiler_params(("parallel",)),
        name="merge",
    )(y_rg, y_m, gates, gates, w_rg, w_m, w_out, h1)


def kernel(x, ffn1_norm, ffn1_w_gate, ffn1_w_up, ffn1_w_down, mix_norm, w_in, rg_conv_w, rg_conv_b, rg_w_a, rg_b_a, rg_w_x, rg_b_x, rg_lambda, m_conv_w, m_conv_b, m_w_q, m_w_k, m_w_v, m_b_i, m_b_f, m_norm, m_skip, w_proj_rg, w_proj_m, w_out, ffn2_norm, ffn2_w_gate, ffn2_w_up, ffn2_w_down, final_norm):
    bsz, s, d = x.shape
    assert (bsz, s, d) == (1, SEQ, D_MODEL) and ffn1_norm.shape[0] == 1
    bf = lambda w: w.astype(BF16)
    h = x[0]

    h, u_mix = _ffn(h, ffn1_norm, ffn1_w_gate[0], ffn1_w_up[0], ffn1_w_down[0], mix_norm,
                    final_norm=False)

    w_in_t = jnp.swapaxes(w_in[0], 0, 1)
    w_if_t = jnp.pad(w_in_t[ROW_IF:], ((0, LANES - 2 * M_HEADS), (0, 0)))
    gates = _gate_proj(u_mix, w_in_t)
    w_mix_t = _cast_rows(w_in_t, ROW_GATES)

    per_block = lambda v: v.reshape(-1, RG_BLOCKS, RG_BLOCK).swapaxes(0, 1)
    y_rg = _rg_mixer(u_mix, w_mix_t, per_block(rg_conv_w[0]), per_block(rg_conv_b),
                     bf(rg_w_a[0]), rg_b_a[0][:, None, :], bf(rg_w_x[0]), rg_b_x[0][:, None, :],
                     per_block(rg_lambda))

    b_if = jnp.pad(jnp.concatenate([m_b_i, m_b_f], axis=1), ((0, 0), (0, LANES - 2 * M_HEADS)))
    y_m = _m_mixer(u_mix, w_mix_t, bf(w_if_t), m_conv_w[0], m_conv_b,
                   bf(m_w_q[0]), bf(m_w_k[0]), bf(m_w_v[0]), b_if, m_norm, m_skip)

    h = _merge(y_rg, y_m, gates, bf(w_proj_rg[0]), bf(w_proj_m[0]), bf(w_out[0]), h)

    h = _ffn(h, ffn2_norm, ffn2_w_gate[0], ffn2_w_up[0], ffn2_w_down[0], final_norm[None, :],
             final_norm=True)
    return h[None]
```

```python
import functools

import jax
import jax.numpy as jnp
from jax import lax
from jax.experimental import pallas as pl
from jax.experimental.pallas import tpu as pltpu

D_MODEL = 2048
SEQ = 8192
D_FF = 5632
CONV_WIDTH = 4
RG_WIDTH = 2048
RG_BLOCKS = 8
RG_BLOCK = RG_WIDTH // RG_BLOCKS
RG_C = 8.0
M_HEADS = 8
M_HEAD_DIM = 256
M_WIDTH = M_HEADS * M_HEAD_DIM
M_CHUNK = 128
NORM_EPS = 1e-6

LANES = 128
SUBLANES = 8
V7X_VMEM_BYTES = 64 * 1024 * 1024
VMEM_LIMIT_BYTES = (V7X_VMEM_BYTES // 16) * 15

ROW_RG = 0
ROW_M = 2 * RG_WIDTH
ROW_GATES = ROW_M + 2 * M_WIDTH
ROW_IF = ROW_GATES + 2 * D_MODEL

FFN_TM, FFN_TF = 1024, 256
FFN_TF_BF16 = 256
FFN_ROW_CHUNK = 128
GATE_TM, GATE_TN = 2048, 512
MIX_TM = 256
M_PROJ_PIECE = 512
CAST_ROWS = 512
MERGE_TM = 256

BF16 = jnp.bfloat16
F32 = jnp.float32
_NT_DIMS = (((1,), (1,)), ((), ()))


def _rms_norm(x, g):
    ms = jnp.mean(x * x, axis=-1, keepdims=True)
    return x * lax.rsqrt(ms + NORM_EPS) * g


def _softplus(z):
    return jnp.maximum(z, 0.0) + jnp.log1p(jnp.exp(-jnp.abs(z)))


def _compiler_params(semantics):
    return pltpu.CompilerParams(dimension_semantics=semantics,
                                vmem_limit_bytes=VMEM_LIMIT_BYTES)


def _resident(shape, col_block=0):
    index = (0,) * (len(shape) - 1) + (col_block,)
    return pl.BlockSpec(shape, lambda *_: index, pipeline_mode=pl.Buffered(1))


def _cast_kernel(x_ref, o_ref):
    o_ref[...] = x_ref[...].T.astype(BF16)


def _cast_rows(w, n_rows):
    d = w.shape[1]
    return pl.pallas_call(
        _cast_kernel,
        grid=(n_rows // CAST_ROWS,),
        in_specs=[pl.BlockSpec((CAST_ROWS, d), lambda i: (i, 0))],
        out_specs=pl.BlockSpec((d, CAST_ROWS), lambda i: (0, i)),
        out_shape=jax.ShapeDtypeStruct((d, n_rows), BF16),
        compiler_params=_compiler_params(("parallel",)),
        name="cast_rows",
    )(w)


def _ffn_kernel(*refs, final_norm):
    if final_norm:
        x_ref, g_ref, wg_ref, wu_ref, wd_ref, eg_ref, o_ref, u_ref = refs
    else:
        x_ref, g_ref, wg_ref, wu_ref, wd_ref, eg_ref, o_ref, un_ref, u_ref = refs
    j = pl.program_id(1)

    row_chunks = [pl.ds(r, FFN_ROW_CHUNK) for r in range(0, x_ref.shape[0], FFN_ROW_CHUNK)]

    @pl.when(j == 0)
    def _():
        for rows in row_chunks:
            u_ref[rows, :] = _rms_norm(x_ref[rows, :], g_ref[...]).astype(BF16)
        o_ref[...] = jnp.zeros_like(o_ref)

    u = u_ref[...]
    gate = jnp.dot(u, wg_ref[...].astype(BF16), preferred_element_type=F32)
    up = jnp.dot(u, wu_ref[...].astype(BF16), preferred_element_type=F32)
    act = (gate * jax.nn.sigmoid(gate) * up).astype(BF16)
    o_ref[...] += jnp.dot(act, wd_ref[...].astype(BF16), preferred_element_type=F32)

    @pl.when(j == pl.num_programs(1) - 1)
    def _():
        for rows in row_chunks:
            h = x_ref[rows, :] + 0.5 * o_ref[rows, :]
            if final_norm:
                o_ref[rows, :] = _rms_norm(h, eg_ref[...])
            else:
                o_ref[rows, :] = h
                un_ref[rows, :] = _rms_norm(h, eg_ref[...]).astype(BF16)


def _ffn(x, norm_g, wg, wu, wd, extra_g, final_norm):
    s, d = x.shape
    f = wg.shape[1]
    tm = FFN_TM
    tf = FFN_TF_BF16 if wg.dtype == BF16 else FFN_TF
    row_spec = pl.BlockSpec((tm, d), lambda i, j: (i, 0))
    gain_spec = pl.BlockSpec((1, d), lambda i, j: (0, 0))
    out_specs, out_shape = row_spec, jax.ShapeDtypeStruct((s, d), F32)
    if not final_norm:
        out_specs = [row_spec, row_spec]
        out_shape = [out_shape, jax.ShapeDtypeStruct((s, d), BF16)]
    return pl.pallas_call(
        functools.partial(_ffn_kernel, final_norm=final_norm),
        grid=(s // tm, f // tf),
        in_specs=[
            row_spec,
            gain_spec,
            pl.BlockSpec((d, tf), lambda i, j: (0, j)),
            pl.BlockSpec((d, tf), lambda i, j: (0, j)),
            pl.BlockSpec((tf, d), lambda i, j: (j, 0)),
            gain_spec,
        ],
        out_specs=out_specs,
        out_shape=out_shape,
        scratch_shapes=[pltpu.VMEM((tm, d), BF16)],
        compiler_params=_compiler_params(("parallel", "arbitrary")),
        name="ffn_final" if final_norm else "ffn",
    )(x, norm_g, wg, wu, wd, extra_g)


def _gate_proj_kernel(u_ref, wt_ref, o_ref):
    o_ref[...] = lax.dot_general(u_ref[...], wt_ref[...].astype(BF16), _NT_DIMS,
                                 preferred_element_type=F32)


def _gate_proj(u, w_in_t):
    s, d = u.shape
    n = 2 * D_MODEL
    tm, tn = GATE_TM, GATE_TN
    row0 = ROW_GATES // tn
    return pl.pallas_call(
        _gate_proj_kernel,
        grid=(s // tm, n // tn),
        in_specs=[
            pl.BlockSpec((tm, d), lambda i, j: (i, 0)),
            pl.BlockSpec((tn, d), lambda i, j: (row0 + j, 0)),
        ],
        out_specs=pl.BlockSpec((tm, tn), lambda i, j: (i, j)),
        out_shape=jax.ShapeDtypeStruct((s, n), F32),
        compiler_params=_compiler_params(("parallel", "arbitrary")),
        name="gate_proj",
    )(u, w_in_t)


def _causal_conv(x, xbuf_ref, cols, cw, cb, rows):
    xbuf_ref[pl.ds(SUBLANES, rows), cols] = x
    out = cb + cw[CONV_WIDTH - 1:CONV_WIDTH, :] * x
    for back in range(1, CONV_WIDTH):
        tap = CONV_WIDTH - 1 - back
        out = out + cw[tap:tap + 1, :] * xbuf_ref[pl.ds(SUBLANES - back, rows), cols]
    xbuf_ref[pl.ds(0, SUBLANES), cols] = xbuf_ref[pl.ds(rows, SUBLANES), cols]
    return out


def _skewed_steps(n_tiles):
    return (n_tiles + 1,
            lambda s: (jnp.minimum(s, n_tiles - 1), 0),
            lambda s: (jnp.maximum(s - 1, 0), 0))


def _rg_front(n, u, w_ref, dst_ref):
    cols_x = pl.ds(n * RG_BLOCK, RG_BLOCK)
    cols_g = pl.ds(RG_WIDTH + n * RG_BLOCK, RG_BLOCK)
    dst_ref[0, n] = jnp.dot(u, w_ref[:, cols_x], preferred_element_type=F32)
    dst_ref[1, n] = jnp.dot(u, w_ref[:, cols_g], preferred_element_type=F32)


def _rg_back(n, src_ref, obuf_ref, xbuf_ref, carry_ref, stage_ref, cw_ref, cb_ref, wa_ref, ba_ref,
             wx_ref, bx_ref, lam_ref):
    tm = obuf_ref.shape[1]
    group = (SUBLANES, RG_BLOCK)
    row = lax.broadcasted_iota(jnp.int32, group, 0)
    xc = _causal_conv(src_ref[0, n], xbuf_ref.at[n], slice(None), cw_ref[n], cb_ref[n], tm)
    xcb = xc.astype(BF16)
    stage_ref[0, n] = xc
    stage_ref[1, n] = jnp.dot(xcb, wa_ref[n], preferred_element_type=F32)
    stage_ref[2, n] = jnp.dot(xcb, wx_ref[n], preferred_element_type=F32)
    scale = jnp.broadcast_to(-RG_C * _softplus(-lam_ref[n]), group)
    ba = jnp.broadcast_to(ba_ref[n], group)
    bx = jnp.broadcast_to(bx_ref[n], group)

    carry = carry_ref[n]
    out_rows = 2 * SUBLANES
    hs = []
    for r0 in range(0, tm, SUBLANES):
        rows = pl.ds(r0, SUBLANES)
        xg = stage_ref[0, n, rows, :]
        r = jax.nn.sigmoid(stage_ref[1, n, rows, :] + ba)
        i = jax.nn.sigmoid(stage_ref[2, n, rows, :] + bx)
        log_a = r * scale
        a = jnp.exp(log_a)
        v = -jnp.tanh(log_a) * (a * a + 1.0)
        b = jnp.where(v > 0.0, v * lax.rsqrt(v), 0.0) * (i * xg)
        for k in (1, 2, 4):
            keep = row >= k
            a_prev = pltpu.roll(a, k, axis=0)
            b_prev = pltpu.roll(b, k, axis=0)
            b = jnp.where(keep, a * b_prev + b, b)
            a = jnp.where(keep, a * a_prev, a)
        h = b + a * carry
        carry = h[SUBLANES - 1:SUBLANES, :]
        hs.append(h * jax.nn.gelu(src_ref[1, n, rows, :]))
        if len(hs) * SUBLANES == out_rows:
            obuf_ref[n, pl.ds(r0 + SUBLANES - out_rows, out_rows), :] = (
                jnp.concatenate(hs, axis=0).astype(BF16))
            hs = []
    carry_ref[n] = carry


def _rg_mixer_kernel(u_ref, wt_ref, cw_ref, cb_ref, wa_ref, ba_ref, wx_ref, bx_ref, lam_ref,
                     side0_ref, side1_ref, side2_ref,
                     o_ref, cast0_ref, cast1_ref, cast2_ref,
                     p0_ref, p1_ref, xbuf_ref, carry_ref, obuf_ref, stage_ref):
    s = pl.program_id(0)

    for side_ref, cast_ref in ((side0_ref, cast0_ref), (side1_ref, cast1_ref),
                               (side2_ref, cast2_ref)):
        cast_ref[...] = side_ref[...].astype(BF16)

    @pl.when(s == 0)
    def _():
        p1_ref[...] = jnp.zeros_like(p1_ref)

    @pl.when(s <= 1)
    def _():
        xbuf_ref[:, pl.ds(0, SUBLANES), :] = jnp.zeros((RG_BLOCKS, SUBLANES, RG_BLOCK), F32)
        carry_ref[...] = jnp.zeros_like(carry_ref)

    def step(dst_ref, src_ref):
        u = u_ref[...]
        for n in range(RG_BLOCKS):
            _rg_front(n, u, wt_ref, dst_ref)
            _rg_back(n, src_ref, obuf_ref, xbuf_ref, carry_ref, stage_ref, cw_ref, cb_ref, wa_ref,
                     ba_ref, wx_ref, bx_ref, lam_ref)
            o_ref[:, pl.ds(n * RG_BLOCK, RG_BLOCK)] = obuf_ref[n]

    pl.when(s % 2 == 0)(lambda: step(p0_ref, p1_ref))
    pl.when(s % 2 == 1)(lambda: step(p1_ref, p0_ref))


def _rg_mixer(u, w_t, conv_w, conv_b, w_a, b_a, w_x, b_x, lam, side):
    s, d = u.shape
    tm = MIX_TM
    n_tiles = s // tm
    steps, cur_map, prev_map = _skewed_steps(n_tiles)
    blk = RG_BLOCK
    side_specs = [pl.BlockSpec((w.shape[0] // n_tiles, w.shape[1]), cur_map) for w in side]
    return pl.pallas_call(
        _rg_mixer_kernel,
        grid=(steps,),
        in_specs=[
            pl.BlockSpec((tm, d), cur_map),
            _resident((d, 2 * RG_WIDTH), ROW_RG // (2 * RG_WIDTH)),
            _resident((RG_BLOCKS, CONV_WIDTH, blk)),
            _resident((RG_BLOCKS, 1, blk)),
            _resident((RG_BLOCKS, blk, blk)),
            _resident((RG_BLOCKS, 1, blk)),
            _resident((RG_BLOCKS, blk, blk)),
            _resident((RG_BLOCKS, 1, blk)),
            _resident((RG_BLOCKS, 1, blk)),
        ] + side_specs,
        out_specs=[pl.BlockSpec((tm, RG_WIDTH), prev_map)] + side_specs,
        out_shape=([jax.ShapeDtypeStruct((s, RG_WIDTH), BF16)]
                   + [jax.ShapeDtypeStruct(w.shape, BF16) for w in side]),
        scratch_shapes=[
            pltpu.VMEM((2, RG_BLOCKS, tm, blk), F32),
            pltpu.VMEM((2, RG_BLOCKS, tm, blk), F32),
            pltpu.VMEM((RG_BLOCKS, tm + SUBLANES, blk), F32),
            pltpu.VMEM((RG_BLOCKS, 1, blk), F32),
            pltpu.VMEM((RG_BLOCKS, tm, blk), BF16),
            pltpu.VMEM((3, RG_BLOCKS, tm, blk), F32),
        ],
        compiler_params=_compiler_params(("arbitrary",)),
        name="rg_mixer",
    )(u, w_t, conv_w, conv_b, w_a, b_a, w_x, b_x, lam, *side)


def _mlstm_tile_qkv(ux, xbuf_ref, cw_ref, cb_ref, wq_ref, wk_ref, wv_ref):
    tm, D = ux.shape[0], M_HEAD_DIM
    conv = _causal_conv(ux, xbuf_ref, slice(None), cw_ref[...], cb_ref[...], tm)
    xc = conv * jax.nn.sigmoid(conv)
    heads = range(M_HEADS)
    sls = [slice(h * D, (h + 1) * D) for h in heads]
    xcb = [xc[:, sl].astype(BF16) for sl in sls]
    q = [jnp.dot(xcb[h], wq_ref[h], preferred_element_type=F32) for h in heads]
    kk = [jnp.dot(xcb[h], wk_ref[h], preferred_element_type=F32) * (M_HEAD_DIM ** -0.5)
          for h in heads]
    v = [jnp.dot(ux[:, sls[h]].astype(BF16), wv_ref[h], preferred_element_type=F32)
         for h in heads]
    return xc, q, kk, v


def _mlstm_chunk(xc, q, kk, v, mo, gif, o_ref, rows, c_ref, n_ref, m_ref, bif_ref, ng_ref,
                 skip_ref):
    L, D = M_CHUNK, M_HEAD_DIM

    gt = (gif + bif_ref[...]).T
    log_i = gt[0:M_HEADS, :]
    log_f = -_softplus(-gt[M_HEADS:2 * M_HEADS, :])
    lane = lax.broadcasted_iota(jnp.int32, (M_HEADS, L), 1)
    b = log_f
    k = 1
    while k < L:
        b = b + jnp.where(lane >= k, pltpu.roll(b, k, axis=1), 0.0)
        k *= 2
    g = b[:, L - 1:L]
    r = log_i - b
    a = g + r
    m_loc = jnp.max(a, axis=1, keepdims=True)
    w = jnp.exp(a - m_loc)
    m_prev = m_ref[:, 0:1]
    m_new = jnp.maximum(g + m_prev, m_loc)
    s_old = jnp.exp(g + m_prev - m_new)
    s_loc = jnp.exp(m_loc - m_new)
    m_ref[...] = jnp.broadcast_to(m_new, m_ref.shape)

    rmax = r
    k = 1
    while k < L:
        rmax = jnp.maximum(rmax, jnp.where(lane >= k, pltpu.roll(rmax, k, axis=1), -jnp.inf))
        k *= 2
    cm = jnp.maximum(m_prev, rmax)
    iw = jnp.exp(m_prev - cm)
    em = jnp.exp(-(b + cm))

    cols = jnp.concatenate([cm, w, iw, em, jnp.zeros((LANES - 4 * M_HEADS, L), F32)], axis=0).T

    tri = (lax.broadcasted_iota(jnp.int32, (L, L), 0)
           >= lax.broadcasted_iota(jnp.int32, (L, L), 1))

    heads = range(M_HEADS)
    sls = [slice(h * D, (h + 1) * D) for h in heads]

    qb = [x.astype(BF16) for x in q]
    kb = [x.astype(BF16) for x in kk]
    qk = [lax.dot_general(qb[h], kb[h], _NT_DIMS, preferred_element_type=F32) for h in heads]

    col = lambda which, h: cols[:, which * M_HEADS + h:which * M_HEADS + h + 1]
    cm_col = [col(0, h) for h in heads]
    w_col = [col(1, h) for h in heads]
    inter_w = [col(2, h) for h in heads]
    em_col = [col(3, h) for h in heads]
    c_prev = [c_ref[h] for h in heads]
    n_prev = [n_ref[h] for h in heads]
    qc = [jnp.dot(qb[h], c_prev[h].astype(BF16), preferred_element_type=F32) for h in heads]
    c_loc = [jnp.dot(kk[h].T.astype(BF16), (w_col[h] * v[h]).astype(BF16),
                     preferred_element_type=F32) for h in heads]

    scores = [qk[h] * jnp.exp(jnp.where(tri, r[h:h + 1, :] - cm_col[h], -jnp.inf))
              for h in heads]
    sv = [jnp.dot(scores[h].astype(BF16), v[h].astype(BF16), preferred_element_type=F32)
          for h in heads]

    for h in heads:
        s_old_h = s_old[h:h + 1, :]
        s_loc_h = s_loc[h:h + 1, :]
        c_ref[h] = s_old_h * c_prev[h] + s_loc_h * c_loc[h]
        n_loc = jnp.sum(w_col[h] * kk[h], axis=0, keepdims=True)
        n_ref[h] = s_old_h * n_prev[h] + s_loc_h * n_loc

    for h in heads:
        sl = sls[h]
        num = sv[h] + inter_w[h] * qc[h]
        den = (jnp.sum(scores[h], axis=1, keepdims=True)
               + inter_w[h] * jnp.sum(q[h] * n_prev[h], axis=1, keepdims=True))
        hh = num * (1.0 / jnp.maximum(jnp.abs(den), em_col[h]))
        mu = jnp.mean(hh, axis=1, keepdims=True)
        cen = hh - mu
        var = jnp.mean(cen * cen, axis=1, keepdims=True)
        y = cen * lax.rsqrt(var + NORM_EPS) * ng_ref[:, sl] + skip_ref[:, sl] * xc[:, sl]
        o_ref[rows, sl] = (jax.nn.sigmoid(mo[:, sl]) * y).astype(o_ref.dtype)


def _m_mixer_kernel(u_ref, wt_ref, wift_ref, cw_ref, cb_ref, wq_ref, wk_ref, wv_ref, bif_ref,
                    ng_ref, skip_ref, o_ref, p0_ref, p1_ref, g0_ref, g1_ref, xbuf_ref, c_ref,
                    n_ref, m_ref):
    s = pl.program_id(0)

    @pl.when(s == 0)
    def _():
        p1_ref[...] = jnp.zeros_like(p1_ref)
        g1_ref[...] = jnp.zeros_like(g1_ref)

    @pl.when(s <= 1)
    def _():
        xbuf_ref[pl.ds(0, SUBLANES), :] = jnp.zeros((SUBLANES, M_WIDTH), F32)
        c_ref[...] = jnp.zeros_like(c_ref)
        n_ref[...] = jnp.zeros_like(n_ref)
        m_ref[...] = jnp.zeros_like(m_ref)

    def step(dst_ref, gdst_ref, src_ref, gsrc_ref):
        u = u_ref[...]
        for c0 in range(0, 2 * M_WIDTH, M_PROJ_PIECE):
            cols = pl.ds(c0, M_PROJ_PIECE)
            dst_ref[:, cols] = jnp.dot(u, wt_ref[:, cols], preferred_element_type=F32)
        gdst_ref[...] = lax.dot_general(u, wift_ref[...], _NT_DIMS, preferred_element_type=F32)

        xc, q, kk, v = _mlstm_tile_qkv(src_ref[:, pl.ds(0, M_WIDTH)], xbuf_ref, cw_ref, cb_ref,
                                       wq_ref, wk_ref, wv_ref)
        for r0 in range(0, o_ref.shape[0], M_CHUNK):
            rows = pl.ds(r0, M_CHUNK)
            part = lambda xs: [x[r0:r0 + M_CHUNK, :] for x in xs]
            _mlstm_chunk(xc[r0:r0 + M_CHUNK, :], part(q), part(kk), part(v),
                         src_ref[rows, pl.ds(M_WIDTH, M_WIDTH)], gsrc_ref[rows, :], o_ref, rows,
                         c_ref, n_ref, m_ref, bif_ref, ng_ref, skip_ref)

    pl.when(s % 2 == 0)(lambda: step(p0_ref, g0_ref, p1_ref, g1_ref))
    pl.when(s % 2 == 1)(lambda: step(p1_ref, g1_ref, p0_ref, g0_ref))


def _m_mixer(u, w_t, w_if_t, conv_w, conv_b, w_q, w_k, w_v, b_if, norm_g, skip):
    s, d = u.shape
    tm = MIX_TM
    steps, cur_map, prev_map = _skewed_steps(s // tm)
    head_w = (M_HEADS, M_HEAD_DIM, M_HEAD_DIM)
    return pl.pallas_call(
        _m_mixer_kernel,
        grid=(steps,),
        in_specs=[
            pl.BlockSpec((tm, d), cur_map),
            _resident((d, 2 * M_WIDTH), ROW_M // (2 * M_WIDTH)),
            _resident((LANES, d)),
            _resident((CONV_WIDTH, M_WIDTH)),
            _resident((1, M_WIDTH)),
            _resident(head_w),
            _resident(head_w),
            _resident(head_w),
            _resident((1, LANES)),
            _resident((1, M_WIDTH)),
            _resident((1, M_WIDTH)),
        ],
        out_specs=pl.BlockSpec((tm, M_WIDTH), prev_map),
        out_shape=jax.ShapeDtypeStruct((s, M_WIDTH), BF16),
        scratch_shapes=[
            pltpu.VMEM((tm, 2 * M_WIDTH), F32),
            pltpu.VMEM((tm, 2 * M_WIDTH), F32),
            pltpu.VMEM((tm, LANES), F32),
            pltpu.VMEM((tm, LANES), F32),
            pltpu.VMEM((tm + SUBLANES, M_WIDTH), F32),
            pltpu.VMEM(head_w, F32),
            pltpu.VMEM((M_HEADS, 1, M_HEAD_DIM), F32),
            pltpu.VMEM((M_HEADS, LANES), F32),
        ],
        compiler_params=_compiler_params(("arbitrary",)),
        name="m_mixer",
    )(u, w_t, w_if_t, conv_w, conv_b, w_q, w_k, w_v, b_if, norm_g, skip)


def _merge_kernel(yrg_ref, ym_ref, grg_ref, gm_ref, wrg_ref, wm_ref, wout_ref, h_ref, o_ref):
    merged = (jax.nn.sigmoid(grg_ref[...])
              * jnp.dot(yrg_ref[...], wrg_ref[...], preferred_element_type=F32)
              + jax.nn.sigmoid(gm_ref[...])
              * jnp.dot(ym_ref[...], wm_ref[...], preferred_element_type=F32))
    o_ref[...] = h_ref[...] + jnp.dot(merged.astype(BF16), wout_ref[...],
                                      preferred_element_type=F32)


def _merge(y_rg, y_m, gates, w_rg, w_m, w_out, h1):
    s, d = h1.shape
    tm = MERGE_TM
    return pl.pallas_call(
        _merge_kernel,
        grid=(s // tm,),
        in_specs=[
            pl.BlockSpec((tm, RG_WIDTH), lambda i: (i, 0)),
            pl.BlockSpec((tm, M_WIDTH), lambda i: (i, 0)),
            pl.BlockSpec((tm, d), lambda i: (i, 0)),
            pl.BlockSpec((tm, d), lambda i: (i, 1)),
            _resident((RG_WIDTH, d)),
            _resident((M_WIDTH, d)),
            _resident((d, d)),
            pl.BlockSpec((tm, d), lambda i: (i, 0)),
        ],
        out_specs=pl.BlockSpec((tm, d), lambda i: (i, 0)),
        out_shape=jax.ShapeDtypeStruct((s, d), F32),
        compiler_params=_compiler_params(("parallel",)),
        name="merge",
    )(y_rg, y_m, gates, gates, w_rg, w_m, w_out, h1)


def kernel(x, ffn1_norm, ffn1_w_gate, ffn1_w_up, ffn1_w_down, mix_norm, w_in, rg_conv_w, rg_conv_b, rg_w_a, rg_b_a, rg_w_x, rg_b_x, rg_lambda, m_conv_w, m_conv_b, m_w_q, m_w_k, m_w_v, m_b_i, m_b_f, m_norm, m_skip, w_proj_rg, w_proj_m, w_out, ffn2_norm, ffn2_w_gate, ffn2_w_up, ffn2_w_down, final_norm):
    bsz, s, d = x.shape
    assert (bsz, s, d) == (1, SEQ, D_MODEL) and ffn1_norm.shape[0] == 1
    bf = lambda w: w.astype(BF16)
    h = x[0]

    h, u_mix = _ffn(h, ffn1_norm, ffn1_w_gate[0], ffn1_w_up[0], ffn1_w_down[0], mix_norm,
                    final_norm=False)

    w_in_t = jnp.swapaxes(w_in[0], 0, 1)
    w_if_t = jnp.pad(w_in_t[ROW_IF:], ((0, LANES - 2 * M_HEADS), (0, 0)))
    gates = _gate_proj(u_mix, w_in_t)
    w_mix_t = _cast_rows(w_in_t, ROW_GATES)

    per_block = lambda v: v.reshape(-1, RG_BLOCKS, RG_BLOCK).swapaxes(0, 1)
    y_rg, w2_gate, w2_up, w2_down = _rg_mixer(
        u_mix, w_mix_t, per_block(rg_conv_w[0]), per_block(rg_conv_b),
        bf(rg_w_a[0]), rg_b_a[0][:, None, :], bf(rg_w_x[0]), rg_b_x[0][:, None, :],
        per_block(rg_lambda), side=(ffn2_w_gate[0], ffn2_w_up[0], ffn2_w_down[0]))

    b_if = jnp.pad(jnp.concatenate([m_b_i, m_b_f], axis=1), ((0, 0), (0, LANES - 2 * M_HEADS)))
    y_m = _m_mixer(u_mix, w_mix_t, bf(w_if_t), m_conv_w[0], m_conv_b,
                   bf(m_w_q[0]), bf(m_w_k[0]), bf(m_w_v[0]), b_if, m_norm, m_skip)

    h = _merge(y_rg, y_m, gates, bf(w_proj_rg[0]), bf(w_proj_m[0]), bf(w_out[0]), h)

    h = _ffn(h, ffn2_norm, w2_gate, w2_up, w2_down, final_norm[None, :], final_norm=True)
    return h[None]
```

```python
import functools

import jax
import jax.numpy as jnp
from jax import lax
from jax.experimental import pallas as pl
from jax.experimental.pallas import tpu as pltpu

D_MODEL = 2048
SEQ = 8192
D_FF = 5632
CONV_WIDTH = 4
RG_WIDTH = 2048
RG_BLOCKS = 8
RG_BLOCK = RG_WIDTH // RG_BLOCKS
RG_C = 8.0
M_HEADS = 8
M_HEAD_DIM = 256
M_WIDTH = M_HEADS * M_HEAD_DIM
M_CHUNK = 128
NORM_EPS = 1e-6

LANES = 128
SUBLANES = 8
V7X_VMEM_BYTES = 64 * 1024 * 1024
VMEM_LIMIT_BYTES = (V7X_VMEM_BYTES // 16) * 15
VMEM_LIMIT_WIDE_BYTES = (V7X_VMEM_BYTES // 64) * 63

ROW_RG = 0
ROW_M = 2 * RG_WIDTH
ROW_GATES = ROW_M + 2 * M_WIDTH
ROW_IF = ROW_GATES + 2 * D_MODEL

FFN_TM, FFN_TF = 1024, 256
FFN_TF_BF16 = 512
FFN_ROW_CHUNK = 128
GATE_TM, GATE_TN = 2048, 512
MIX_TM = 256
M_PROJ_PIECE = 512
CAST_ROWS = 512
MERGE_TM = 256

BF16 = jnp.bfloat16
F32 = jnp.float32
_NT_DIMS = (((1,), (1,)), ((), ()))


def _rms_norm(x, g):
    ms = jnp.mean(x * x, axis=-1, keepdims=True)
    return x * lax.rsqrt(ms + NORM_EPS) * g


def _softplus(z):
    return jnp.maximum(z, 0.0) + jnp.log1p(jnp.exp(-jnp.abs(z)))


def _compiler_params(semantics, vmem_limit=None):
    return pltpu.CompilerParams(dimension_semantics=semantics,
                                vmem_limit_bytes=vmem_limit or VMEM_LIMIT_BYTES)


def _resident(shape, col_block=0):
    index = (0,) * (len(shape) - 1) + (col_block,)
    return pl.BlockSpec(shape, lambda *_: index, pipeline_mode=pl.Buffered(1))


def _cast_kernel(x_ref, o_ref):
    o_ref[...] = x_ref[...].T.astype(BF16)


def _cast_rows(w, n_rows):
    d = w.shape[1]
    return pl.pallas_call(
        _cast_kernel,
        grid=(n_rows // CAST_ROWS,),
        in_specs=[pl.BlockSpec((CAST_ROWS, d), lambda i: (i, 0))],
        out_specs=pl.BlockSpec((d, CAST_ROWS), lambda i: (0, i)),
        out_shape=jax.ShapeDtypeStruct((d, n_rows), BF16),
        compiler_params=_compiler_params(("parallel",)),
        name="cast_rows",
    )(w)


def _ffn_kernel(*refs, final_norm):
    if final_norm:
        x_ref, g_ref, wg_ref, wu_ref, wd_ref, eg_ref, o_ref, u_ref = refs
    else:
        x_ref, g_ref, wg_ref, wu_ref, wd_ref, eg_ref, o_ref, un_ref, u_ref = refs
    j = pl.program_id(1)

    row_chunks = [pl.ds(r, FFN_ROW_CHUNK) for r in range(0, x_ref.shape[0], FFN_ROW_CHUNK)]

    @pl.when(j == 0)
    def _():
        for rows in row_chunks:
            u_ref[rows, :] = _rms_norm(x_ref[rows, :], g_ref[...]).astype(BF16)
        o_ref[...] = jnp.zeros_like(o_ref)

    u = u_ref[...]
    gate = jnp.dot(u, wg_ref[...].astype(BF16), preferred_element_type=F32)
    up = jnp.dot(u, wu_ref[...].astype(BF16), preferred_element_type=F32)
    act = (gate * jax.nn.sigmoid(gate) * up).astype(BF16)
    o_ref[...] += jnp.dot(act, wd_ref[...].astype(BF16), preferred_element_type=F32)

    @pl.when(j == pl.num_programs(1) - 1)
    def _():
        for rows in row_chunks:
            h = x_ref[rows, :] + 0.5 * o_ref[rows, :]
            if final_norm:
                o_ref[rows, :] = _rms_norm(h, eg_ref[...])
            else:
                o_ref[rows, :] = h
                un_ref[rows, :] = _rms_norm(h, eg_ref[...]).astype(BF16)


def _ffn(x, norm_g, wg, wu, wd, extra_g, final_norm):
    s, d = x.shape
    f = wg.shape[1]
    tm = FFN_TM
    bf16_weights = wg.dtype == BF16
    tf = FFN_TF_BF16 if bf16_weights else FFN_TF
    vmem_limit = VMEM_LIMIT_WIDE_BYTES if bf16_weights else VMEM_LIMIT_BYTES
    row_spec = pl.BlockSpec((tm, d), lambda i, j: (i, 0))
    gain_spec = pl.BlockSpec((1, d), lambda i, j: (0, 0))
    out_specs, out_shape = row_spec, jax.ShapeDtypeStruct((s, d), F32)
    if not final_norm:
        out_specs = [row_spec, row_spec]
        out_shape = [out_shape, jax.ShapeDtypeStruct((s, d), BF16)]
    return pl.pallas_call(
        functools.partial(_ffn_kernel, final_norm=final_norm),
        grid=(s // tm, f // tf),
        in_specs=[
            row_spec,
            gain_spec,
            pl.BlockSpec((d, tf), lambda i, j: (0, j)),
            pl.BlockSpec((d, tf), lambda i, j: (0, j)),
            pl.BlockSpec((tf, d), lambda i, j: (j, 0)),
            gain_spec,
        ],
        out_specs=out_specs,
        out_shape=out_shape,
        scratch_shapes=[pltpu.VMEM((tm, d), BF16)],
        compiler_params=_compiler_params(("parallel", "arbitrary"), vmem_limit),
        name="ffn_final" if final_norm else "ffn",
    )(x, norm_g, wg, wu, wd, extra_g)


def _gate_proj_kernel(u_ref, wt_ref, o_ref):
    o_ref[...] = lax.dot_general(u_ref[...], wt_ref[...].astype(BF16), _NT_DIMS,
                                 preferred_element_type=F32)


def _gate_proj(u, w_in_t):
    s, d = u.shape
    n = 2 * D_MODEL
    tm, tn = GATE_TM, GATE_TN
    row0 = ROW_GATES // tn
    return pl.pallas_call(
        _gate_proj_kernel,
        grid=(s // tm, n // tn),
        in_specs=[
            pl.BlockSpec((tm, d), lambda i, j: (i, 0)),
            pl.BlockSpec((tn, d), lambda i, j: (row0 + j, 0)),
        ],
        out_specs=pl.BlockSpec((tm, tn), lambda i, j: (i, j)),
        out_shape=jax.ShapeDtypeStruct((s, n), F32),
        compiler_params=_compiler_params(("parallel", "arbitrary")),
        name="gate_proj",
    )(u, w_in_t)


def _causal_conv(x, xbuf_ref, cols, cw, cb, rows):
    xbuf_ref[pl.ds(SUBLANES, rows), cols] = x
    out = cb + cw[CONV_WIDTH - 1:CONV_WIDTH, :] * x
    for back in range(1, CONV_WIDTH):
        tap = CONV_WIDTH - 1 - back
        out = out + cw[tap:tap + 1, :] * xbuf_ref[pl.ds(SUBLANES - back, rows), cols]
    xbuf_ref[pl.ds(0, SUBLANES), cols] = xbuf_ref[pl.ds(rows, SUBLANES), cols]
    return out


def _skewed_steps(n_tiles):
    return (n_tiles + 1,
            lambda s: (jnp.minimum(s, n_tiles - 1), 0),
            lambda s: (jnp.maximum(s - 1, 0), 0))


def _rg_front(n, u, w_ref, dst_ref):
    cols_x = pl.ds(n * RG_BLOCK, RG_BLOCK)
    cols_g = pl.ds(RG_WIDTH + n * RG_BLOCK, RG_BLOCK)
    dst_ref[0, n] = jnp.dot(u, w_ref[:, cols_x], preferred_element_type=F32)
    dst_ref[1, n] = jnp.dot(u, w_ref[:, cols_g], preferred_element_type=F32)


def _rg_back(n, src_ref, obuf_ref, xbuf_ref, carry_ref, stage_ref, cw_ref, cb_ref, wa_ref, ba_ref,
             wx_ref, bx_ref, lam_ref):
    tm = obuf_ref.shape[1]
    group = (SUBLANES, RG_BLOCK)
    row = lax.broadcasted_iota(jnp.int32, group, 0)
    xc = _causal_conv(src_ref[0, n], xbuf_ref.at[n], slice(None), cw_ref[n], cb_ref[n], tm)
    xcb = xc.astype(BF16)
    stage_ref[0, n] = xc
    stage_ref[1, n] = jnp.dot(xcb, wa_ref[n], preferred_element_type=F32)
    stage_ref[2, n] = jnp.dot(xcb, wx_ref[n], preferred_element_type=F32)
    scale = jnp.broadcast_to(-RG_C * _softplus(-lam_ref[n]), group)
    ba = jnp.broadcast_to(ba_ref[n], group)
    bx = jnp.broadcast_to(bx_ref[n], group)

    carry = carry_ref[n]
    out_rows = 2 * SUBLANES
    hs = []
    for r0 in range(0, tm, SUBLANES):
        rows = pl.ds(r0, SUBLANES)
        xg = stage_ref[0, n, rows, :]
        r = jax.nn.sigmoid(stage_ref[1, n, rows, :] + ba)
        i = jax.nn.sigmoid(stage_ref[2, n, rows, :] + bx)
        log_a = r * scale
        a = jnp.exp(log_a)
        v = -jnp.tanh(log_a) * (a * a + 1.0)
        b = jnp.where(v > 0.0, v * lax.rsqrt(v), 0.0) * (i * xg)
        for k in (1, 2, 4):
            keep = row >= k
            a_prev = pltpu.roll(a, k, axis=0)
            b_prev = pltpu.roll(b, k, axis=0)
            b = jnp.where(keep, a * b_prev + b, b)
            a = jnp.where(keep, a * a_prev, a)
        h = b + a * carry
        carry = h[SUBLANES - 1:SUBLANES, :]
        hs.append(h * jax.nn.gelu(src_ref[1, n, rows, :]))
        if len(hs) * SUBLANES == out_rows:
            obuf_ref[n, pl.ds(r0 + SUBLANES - out_rows, out_rows), :] = (
                jnp.concatenate(hs, axis=0).astype(BF16))
            hs = []
    carry_ref[n] = carry


def _rg_mixer_kernel(u_ref, wt_ref, cw_ref, cb_ref, wa_ref, ba_ref, wx_ref, bx_ref, lam_ref,
                     side0_ref, side1_ref, side2_ref,
                     o_ref, cast0_ref, cast1_ref, cast2_ref,
                     p0_ref, p1_ref, xbuf_ref, carry_ref, obuf_ref, stage_ref):
    s = pl.program_id(0)

    for side_ref, cast_ref in ((side0_ref, cast0_ref), (side1_ref, cast1_ref),
                               (side2_ref, cast2_ref)):
        cast_ref[...] = side_ref[...].astype(BF16)

    @pl.when(s == 0)
    def _():
        p1_ref[...] = jnp.zeros_like(p1_ref)

    @pl.when(s <= 1)
    def _():
        xbuf_ref[:, pl.ds(0, SUBLANES), :] = jnp.zeros((RG_BLOCKS, SUBLANES, RG_BLOCK), F32)
        carry_ref[...] = jnp.zeros_like(carry_ref)

    def step(dst_ref, src_ref):
        u = u_ref[...]
        for n in range(RG_BLOCKS):
            _rg_front(n, u, wt_ref, dst_ref)
            _rg_back(n, src_ref, obuf_ref, xbuf_ref, carry_ref, stage_ref, cw_ref, cb_ref, wa_ref,
                     ba_ref, wx_ref, bx_ref, lam_ref)
            o_ref[:, pl.ds(n * RG_BLOCK, RG_BLOCK)] = obuf_ref[n]

    pl.when(s % 2 == 0)(lambda: step(p0_ref, p1_ref))
    pl.when(s % 2 == 1)(lambda: step(p1_ref, p0_ref))


def _rg_mixer(u, w_t, conv_w, conv_b, w_a, b_a, w_x, b_x, lam, side):
    s, d = u.shape
    tm = MIX_TM
    n_tiles = s // tm
    steps, cur_map, prev_map = _skewed_steps(n_tiles)
    blk = RG_BLOCK
    side_specs = [pl.BlockSpec((w.shape[0] // n_tiles, w.shape[1]), cur_map) for w in side]
    return pl.pallas_call(
        _rg_mixer_kernel,
        grid=(steps,),
        in_specs=[
            pl.BlockSpec((tm, d), cur_map),
            _resident((d, 2 * RG_WIDTH), ROW_RG // (2 * RG_WIDTH)),
            _resident((RG_BLOCKS, CONV_WIDTH, blk)),
            _resident((RG_BLOCKS, 1, blk)),
            _resident((RG_BLOCKS, blk, blk)),
            _resident((RG_BLOCKS, 1, blk)),
            _resident((RG_BLOCKS, blk, blk)),
            _resident((RG_BLOCKS, 1, blk)),
            _resident((RG_BLOCKS, 1, blk)),
        ] + side_specs,
        out_specs=[pl.BlockSpec((tm, RG_WIDTH), prev_map)] + side_specs,
        out_shape=([jax.ShapeDtypeStruct((s, RG_WIDTH), BF16)]
                   + [jax.ShapeDtypeStruct(w.shape, BF16) for w in side]),
        scratch_shapes=[
            pltpu.VMEM((2, RG_BLOCKS, tm, blk), F32),
            pltpu.VMEM((2, RG_BLOCKS, tm, blk), F32),
            pltpu.VMEM((RG_BLOCKS, tm + SUBLANES, blk), F32),
            pltpu.VMEM((RG_BLOCKS, 1, blk), F32),
            pltpu.VMEM((RG_BLOCKS, tm, blk), BF16),
            pltpu.VMEM((3, RG_BLOCKS, tm, blk), F32),
        ],
        compiler_params=_compiler_params(("arbitrary",)),
        name="rg_mixer",
    )(u, w_t, conv_w, conv_b, w_a, b_a, w_x, b_x, lam, *side)


def _mlstm_tile_qkv(ux, xbuf_ref, cw_ref, cb_ref, wq_ref, wk_ref, wv_ref):
    tm, D = ux.shape[0], M_HEAD_DIM
    conv = _causal_conv(ux, xbuf_ref, slice(None), cw_ref[...], cb_ref[...], tm)
    xc = conv * jax.nn.sigmoid(conv)
    heads = range(M_HEADS)
    sls = [slice(h * D, (h + 1) * D) for h in heads]
    xcb = [xc[:, sl].astype(BF16) for sl in sls]
    q = [jnp.dot(xcb[h], wq_ref[h], preferred_element_type=F32) for h in heads]
    kk = [jnp.dot(xcb[h], wk_ref[h], preferred_element_type=F32) * (M_HEAD_DIM ** -0.5)
          for h in heads]
    v = [jnp.dot(ux[:, sls[h]].astype(BF16), wv_ref[h], preferred_element_type=F32)
         for h in heads]
    return xc, q, kk, v


def _mlstm_chunk(xc, q, kk, v, mo, gif, o_ref, rows, c_ref, n_ref, m_ref, bif_ref, ng_ref,
                 skip_ref):
    L, D = M_CHUNK, M_HEAD_DIM

    gt = (gif + bif_ref[...]).T
    log_i = gt[0:M_HEADS, :]
    log_f = -_softplus(-gt[M_HEADS:2 * M_HEADS, :])
    lane = lax.broadcasted_iota(jnp.int32, (M_HEADS, L), 1)
    b = log_f
    k = 1
    while k < L:
        b = b + jnp.where(lane >= k, pltpu.roll(b, k, axis=1), 0.0)
        k *= 2
    g = b[:, L - 1:L]
    r = log_i - b
    a = g + r
    m_loc = jnp.max(a, axis=1, keepdims=True)
    w = jnp.exp(a - m_loc)
    m_prev = m_ref[:, 0:1]
    m_new = jnp.maximum(g + m_prev, m_loc)
    s_old = jnp.exp(g + m_prev - m_new)
    s_loc = jnp.exp(m_loc - m_new)
    m_ref[...] = jnp.broadcast_to(m_new, m_ref.shape)

    rmax = r
    k = 1
    while k < L:
        rmax = jnp.maximum(rmax, jnp.where(lane >= k, pltpu.roll(rmax, k, axis=1), -jnp.inf))
        k *= 2
    cm = jnp.maximum(m_prev, rmax)
    iw = jnp.exp(m_prev - cm)
    em = jnp.exp(-(b + cm))

    cols = jnp.concatenate([cm, w, iw, em, jnp.zeros((LANES - 4 * M_HEADS, L), F32)], axis=0).T

    tri = (lax.broadcasted_iota(jnp.int32, (L, L), 0)
           >= lax.broadcasted_iota(jnp.int32, (L, L), 1))

    heads = range(M_HEADS)
    sls = [slice(h * D, (h + 1) * D) for h in heads]

    qb = [x.astype(BF16) for x in q]
    kb = [x.astype(BF16) for x in kk]
    qk = [lax.dot_general(qb[h], kb[h], _NT_DIMS, preferred_element_type=F32) for h in heads]

    col = lambda which, h: cols[:, which * M_HEADS + h:which * M_HEADS + h + 1]
    cm_col = [col(0, h) for h in heads]
    w_col = [col(1, h) for h in heads]
    inter_w = [col(2, h) for h in heads]
    em_col = [col(3, h) for h in heads]
    c_prev = [c_ref[h] for h in heads]
    n_prev = [n_ref[h] for h in heads]
    qc = [jnp.dot(qb[h], c_prev[h].astype(BF16), preferred_element_type=F32) for h in heads]
    c_loc = [jnp.dot(kk[h].T.astype(BF16), (w_col[h] * v[h]).astype(BF16),
                     preferred_element_type=F32) for h in heads]

    scores = [qk[h] * jnp.exp(jnp.where(tri, r[h:h + 1, :] - cm_col[h], -jnp.inf))
              for h in heads]
    sv = [jnp.dot(scores[h].astype(BF16), v[h].astype(BF16), preferred_element_type=F32)
          for h in heads]

    for h in heads:
        s_old_h = s_old[h:h + 1, :]
        s_loc_h = s_loc[h:h + 1, :]
        c_ref[h] = s_old_h * c_prev[h] + s_loc_h * c_loc[h]
        n_loc = jnp.sum(w_col[h] * kk[h], axis=0, keepdims=True)
        n_ref[h] = s_old_h * n_prev[h] + s_loc_h * n_loc

    for h in heads:
        sl = sls[h]
        num = sv[h] + inter_w[h] * qc[h]
        den = (jnp.sum(scores[h], axis=1, keepdims=True)
               + inter_w[h] * jnp.sum(q[h] * n_prev[h], axis=1, keepdims=True))
        hh = num * (1.0 / jnp.maximum(jnp.abs(den), em_col[h]))
        mu = jnp.mean(hh, axis=1, keepdims=True)
        cen = hh - mu
        var = jnp.mean(cen * cen, axis=1, keepdims=True)
        y = cen * lax.rsqrt(var + NORM_EPS) * ng_ref[:, sl] + skip_ref[:, sl] * xc[:, sl]
        o_ref[rows, sl] = (jax.nn.sigmoid(mo[:, sl]) * y).astype(o_ref.dtype)


def _m_mixer_kernel(u_ref, wt_ref, wift_ref, cw_ref, cb_ref, wq_ref, wk_ref, wv_ref, bif_ref,
                    ng_ref, skip_ref, o_ref, p0_ref, p1_ref, g0_ref, g1_ref, xbuf_ref, c_ref,
                    n_ref, m_ref):
    s = pl.program_id(0)

    @pl.when(s == 0)
    def _():
        p1_ref[...] = jnp.zeros_like(p1_ref)
        g1_ref[...] = jnp.zeros_like(g1_ref)

    @pl.when(s <= 1)
    def _():
        xbuf_ref[pl.ds(0, SUBLANES), :] = jnp.zeros((SUBLANES, M_WIDTH), F32)
        c_ref[...] = jnp.zeros_like(c_ref)
        n_ref[...] = jnp.zeros_like(n_ref)
        m_ref[...] = jnp.zeros_like(m_ref)

    def step(dst_ref, gdst_ref, src_ref, gsrc_ref):
        u = u_ref[...]
        for c0 in range(0, 2 * M_WIDTH, M_PROJ_PIECE):
            cols = pl.ds(c0, M_PROJ_PIECE)
            dst_ref[:, cols] = jnp.dot(u, wt_ref[:, cols], preferred_element_type=F32)
        gdst_ref[...] = lax.dot_general(u, wift_ref[...], _NT_DIMS, preferred_element_type=F32)

        xc, q, kk, v = _mlstm_tile_qkv(src_ref[:, pl.ds(0, M_WIDTH)], xbuf_ref, cw_ref, cb_ref,
                                       wq_ref, wk_ref, wv_ref)
        for r0 in range(0, o_ref.shape[0], M_CHUNK):
            rows = pl.ds(r0, M_CHUNK)
            part = lambda xs: [x[r0:r0 + M_CHUNK, :] for x in xs]
            _mlstm_chunk(xc[r0:r0 + M_CHUNK, :], part(q), part(kk), part(v),
                         src_ref[rows, pl.ds(M_WIDTH, M_WIDTH)], gsrc_ref[rows, :], o_ref, rows,
                         c_ref, n_ref, m_ref, bif_ref, ng_ref, skip_ref)

    pl.when(s % 2 == 0)(lambda: step(p0_ref, g0_ref, p1_ref, g1_ref))
    pl.when(s % 2 == 1)(lambda: step(p1_ref, g1_ref, p0_ref, g0_ref))


def _m_mixer(u, w_t, w_if_t, conv_w, conv_b, w_q, w_k, w_v, b_if, norm_g, skip):
    s, d = u.shape
    tm = MIX_TM
    steps, cur_map, prev_map = _skewed_steps(s // tm)
    head_w = (M_HEADS, M_HEAD_DIM, M_HEAD_DIM)
    return pl.pallas_call(
        _m_mixer_kernel,
        grid=(steps,),
        in_specs=[
            pl.BlockSpec((tm, d), cur_map),
            _resident((d, 2 * M_WIDTH), ROW_M // (2 * M_WIDTH)),
            _resident((LANES, d)),
            _resident((CONV_WIDTH, M_WIDTH)),
            _resident((1, M_WIDTH)),
            _resident(head_w),
            _resident(head_w),
            _resident(head_w),
            _resident((1, LANES)),
            _resident((1, M_WIDTH)),
            _resident((1, M_WIDTH)),
        ],
        out_specs=pl.BlockSpec((tm, M_WIDTH), prev_map),
        out_shape=jax.ShapeDtypeStruct((s, M_WIDTH), BF16),
        scratch_shapes=[
            pltpu.VMEM((tm, 2 * M_WIDTH), F32),
            pltpu.VMEM((tm, 2 * M_WIDTH), F32),
            pltpu.VMEM((tm, LANES), F32),
            pltpu.VMEM((tm, LANES), F32),
            pltpu.VMEM((tm + SUBLANES, M_WIDTH), F32),
            pltpu.VMEM(head_w, F32),
            pltpu.VMEM((M_HEADS, 1, M_HEAD_DIM), F32),
            pltpu.VMEM((M_HEADS, LANES), F32),
        ],
        compiler_params=_compiler_params(("arbitrary",)),
        name="m_mixer",
    )(u, w_t, w_if_t, conv_w, conv_b, w_q, w_k, w_v, b_if, norm_g, skip)


def _merge_kernel(yrg_ref, ym_ref, grg_ref, gm_ref, wrg_ref, wm_ref, wout_ref, h_ref, o_ref):
    merged = (jax.nn.sigmoid(grg_ref[...])
              * jnp.dot(yrg_ref[...], wrg_ref[...], preferred_element_type=F32)
              + jax.nn.sigmoid(gm_ref[...])
              * jnp.dot(ym_ref[...], wm_ref[...], preferred_element_type=F32))
    o_ref[...] = h_ref[...] + jnp.dot(merged.astype(BF16), wout_ref[...],
                                      preferred_element_type=F32)


def _merge(y_rg, y_m, gates, w_rg, w_m, w_out, h1):
    s, d = h1.shape
    tm = MERGE_TM
    return pl.pallas_call(
        _merge_kernel,
        grid=(s // tm,),
        in_specs=[
            pl.BlockSpec((tm, RG_WIDTH), lambda i: (i, 0)),
            pl.BlockSpec((tm, M_WIDTH), lambda i: (i, 0)),
            pl.BlockSpec((tm, d), lambda i: (i, 0)),
            pl.BlockSpec((tm, d), lambda i: (i, 1)),
            _resident((RG_WIDTH, d)),
            _resident((M_WIDTH, d)),
            _resident((d, d)),
            pl.BlockSpec((tm, d), lambda i: (i, 0)),
        ],
        out_specs=pl.BlockSpec((tm, d), lambda i: (i, 0)),
        out_shape=jax.ShapeDtypeStruct((s, d), F32),
        compiler_params=_compiler_params(("parallel",)),
        name="merge",
    )(y_rg, y_m, gates, gates, w_rg, w_m, w_out, h1)


def kernel(x, ffn1_norm, ffn1_w_gate, ffn1_w_up, ffn1_w_down, mix_norm, w_in, rg_conv_w, rg_conv_b, rg_w_a, rg_b_a, rg_w_x, rg_b_x, rg_lambda, m_conv_w, m_conv_b, m_w_q, m_w_k, m_w_v, m_b_i, m_b_f, m_norm, m_skip, w_proj_rg, w_proj_m, w_out, ffn2_norm, ffn2_w_gate, ffn2_w_up, ffn2_w_down, final_norm):
    bsz, s, d = x.shape
    assert (bsz, s, d) == (1, SEQ, D_MODEL) and ffn1_norm.shape[0] == 1
    bf = lambda w: w.astype(BF16)
    h = x[0]

    h, u_mix = _ffn(h, ffn1_norm, ffn1_w_gate[0], ffn1_w_up[0], ffn1_w_down[0], mix_norm,
                    final_norm=False)

    w_in_t = jnp.swapaxes(w_in[0], 0, 1)
    w_if_t = jnp.pad(w_in_t[ROW_IF:], ((0, LANES - 2 * M_HEADS), (0, 0)))
    gates = _gate_proj(u_mix, w_in_t)
    w_mix_t = _cast_rows(w_in_t, ROW_GATES)

    per_block = lambda v: v.reshape(-1, RG_BLOCKS, RG_BLOCK).swapaxes(0, 1)
    y_rg, w2_gate, w2_up, w2_down = _rg_mixer(
        u_mix, w_mix_t, per_block(rg_conv_w[0]), per_block(rg_conv_b),
        bf(rg_w_a[0]), rg_b_a[0][:, None, :], bf(rg_w_x[0]), rg_b_x[0][:, None, :],
        per_block(rg_lambda), side=(ffn2_w_gate[0], ffn2_w_up[0], ffn2_w_down[0]))

    b_if = jnp.pad(jnp.concatenate([m_b_i, m_b_f], axis=1), ((0, 0), (0, LANES - 2 * M_HEADS)))
    y_m = _m_mixer(u_mix, w_mix_t, bf(w_if_t), m_conv_w[0], m_conv_b,
                   bf(m_w_q[0]), bf(m_w_k[0]), bf(m_w_v[0]), b_if, m_norm, m_skip)

    h = _merge(y_rg, y_m, gates, bf(w_proj_rg[0]), bf(w_proj_m[0]), bf(w_out[0]), h)

    h = _ffn(h, ffn2_norm, w2_gate, w2_up, w2_down, final_norm[None, :], final_norm=True)
    return h[None]
```

```python
import functools

import jax
import jax.numpy as jnp
from jax import lax
from jax.experimental import pallas as pl
from jax.experimental.pallas import tpu as pltpu

D_MODEL = 2048
SEQ = 8192
D_FF = 5632
CONV_WIDTH = 4
RG_WIDTH = 2048
RG_BLOCKS = 8
RG_BLOCK = RG_WIDTH // RG_BLOCKS
RG_C = 8.0
M_HEADS = 8
M_HEAD_DIM = 256
M_WIDTH = M_HEADS * M_HEAD_DIM
M_CHUNK = 128
NORM_EPS = 1e-6

LANES = 128
SUBLANES = 8
V7X_VMEM_BYTES = 64 * 1024 * 1024
VMEM_LIMIT_BYTES = (V7X_VMEM_BYTES // 16) * 15
VMEM_LIMIT_WIDE_BYTES = (V7X_VMEM_BYTES // 64) * 63

ROW_RG = 0
ROW_M = 2 * RG_WIDTH
ROW_GATES = ROW_M + 2 * M_WIDTH
ROW_IF = ROW_GATES + 2 * D_MODEL

FFN_TM, FFN_TF = 1024, 256
FFN_TF_BF16 = 512
FFN_ROW_CHUNK = 128
GATE_TM, GATE_TN = 2048, 512
MIX_TM = 256
M_PROJ_PIECE = 512
MERGE_TM = 256

BF16 = jnp.bfloat16
F32 = jnp.float32
_NT_DIMS = (((1,), (1,)), ((), ()))


def _rms_norm(x, g):
    ms = jnp.mean(x * x, axis=-1, keepdims=True)
    return x * lax.rsqrt(ms + NORM_EPS) * g


def _softplus(z):
    return jnp.maximum(z, 0.0) + jnp.log1p(jnp.exp(-jnp.abs(z)))


def _compiler_params(semantics, vmem_limit=None):
    return pltpu.CompilerParams(dimension_semantics=semantics,
                                vmem_limit_bytes=vmem_limit or VMEM_LIMIT_BYTES)


def _resident(shape, col_block=0):
    index = (0,) * (len(shape) - 1) + (col_block,)
    return pl.BlockSpec(shape, lambda *_: index, pipeline_mode=pl.Buffered(1))


def _ffn_kernel(*refs, final_norm):
    if final_norm:
        x_ref, g_ref, wg_ref, wu_ref, wd_ref, eg_ref, o_ref, u_ref = refs
    else:
        x_ref, g_ref, wg_ref, wu_ref, wd_ref, eg_ref, o_ref, un_ref, u_ref = refs
    j = pl.program_id(1)

    row_chunks = [pl.ds(r, FFN_ROW_CHUNK) for r in range(0, x_ref.shape[0], FFN_ROW_CHUNK)]

    @pl.when(j == 0)
    def _():
        for rows in row_chunks:
            u_ref[rows, :] = _rms_norm(x_ref[rows, :], g_ref[...]).astype(BF16)
        o_ref[...] = jnp.zeros_like(o_ref)

    u = u_ref[...]
    gate = jnp.dot(u, wg_ref[...].astype(BF16), preferred_element_type=F32)
    up = jnp.dot(u, wu_ref[...].astype(BF16), preferred_element_type=F32)
    act = (gate * jax.nn.sigmoid(gate) * up).astype(BF16)
    o_ref[...] += jnp.dot(act, wd_ref[...].astype(BF16), preferred_element_type=F32)

    @pl.when(j == pl.num_programs(1) - 1)
    def _():
        for rows in row_chunks:
            h = x_ref[rows, :] + 0.5 * o_ref[rows, :]
            if final_norm:
                o_ref[rows, :] = _rms_norm(h, eg_ref[...])
            else:
                o_ref[rows, :] = h
                un_ref[rows, :] = _rms_norm(h, eg_ref[...]).astype(BF16)


def _ffn(x, norm_g, wg, wu, wd, extra_g, final_norm):
    s, d = x.shape
    f = wg.shape[1]
    tm = FFN_TM
    bf16_weights = wg.dtype == BF16
    tf = FFN_TF_BF16 if bf16_weights else FFN_TF
    vmem_limit = VMEM_LIMIT_WIDE_BYTES if bf16_weights else VMEM_LIMIT_BYTES
    row_spec = pl.BlockSpec((tm, d), lambda i, j: (i, 0))
    gain_spec = pl.BlockSpec((1, d), lambda i, j: (0, 0))
    out_specs, out_shape = row_spec, jax.ShapeDtypeStruct((s, d), F32)
    if not final_norm:
        out_specs = [row_spec, row_spec]
        out_shape = [out_shape, jax.ShapeDtypeStruct((s, d), BF16)]
    return pl.pallas_call(
        functools.partial(_ffn_kernel, final_norm=final_norm),
        grid=(s // tm, f // tf),
        in_specs=[
            row_spec,
            gain_spec,
            pl.BlockSpec((d, tf), lambda i, j: (0, j)),
            pl.BlockSpec((d, tf), lambda i, j: (0, j)),
            pl.BlockSpec((tf, d), lambda i, j: (j, 0)),
            gain_spec,
        ],
        out_specs=out_specs,
        out_shape=out_shape,
        scratch_shapes=[pltpu.VMEM((tm, d), BF16)],
        compiler_params=_compiler_params(("parallel", "arbitrary"), vmem_limit),
        name="ffn_final" if final_norm else "ffn",
    )(x, norm_g, wg, wu, wd, extra_g)


def _gate_proj_kernel(u_ref, wt_ref, side_ref, o_ref, cast_ref):
    o_ref[...] = lax.dot_general(u_ref[...], wt_ref[...].astype(BF16), _NT_DIMS,
                                 preferred_element_type=F32)
    cast_ref[...] = side_ref[...].T.astype(BF16)


def _gate_proj(u, w_in_t):
    s, d = u.shape
    n = 2 * D_MODEL
    tm, tn = GATE_TM, GATE_TN
    row0 = ROW_GATES // tn
    nj = n // tn
    side_rows = ROW_GATES // ((s // tm) * nj)
    return pl.pallas_call(
        _gate_proj_kernel,
        grid=(s // tm, nj),
        in_specs=[
            pl.BlockSpec((tm, d), lambda i, j: (i, 0)),
            pl.BlockSpec((tn, d), lambda i, j: (row0 + j, 0)),
            pl.BlockSpec((side_rows, d), lambda i, j: (i * nj + j, 0)),
        ],
        out_specs=[
            pl.BlockSpec((tm, tn), lambda i, j: (i, j)),
            pl.BlockSpec((d, side_rows), lambda i, j: (0, i * nj + j)),
        ],
        out_shape=[jax.ShapeDtypeStruct((s, n), F32),
                   jax.ShapeDtypeStruct((d, ROW_GATES), BF16)],
        compiler_params=_compiler_params(("arbitrary", "arbitrary")),
        name="gate_proj",
    )(u, w_in_t, w_in_t)


def _causal_conv(x, xbuf_ref, cols, cw, cb, rows):
    xbuf_ref[pl.ds(SUBLANES, rows), cols] = x
    out = cb + cw[CONV_WIDTH - 1:CONV_WIDTH, :] * x
    for back in range(1, CONV_WIDTH):
        tap = CONV_WIDTH - 1 - back
        out = out + cw[tap:tap + 1, :] * xbuf_ref[pl.ds(SUBLANES - back, rows), cols]
    xbuf_ref[pl.ds(0, SUBLANES), cols] = xbuf_ref[pl.ds(rows, SUBLANES), cols]
    return out


def _skewed_steps(n_tiles):
    return (n_tiles + 1,
            lambda s: (jnp.minimum(s, n_tiles - 1), 0),
            lambda s: (jnp.maximum(s - 1, 0), 0))


def _rg_front(n, u, w_ref, dst_ref):
    cols_x = pl.ds(n * RG_BLOCK, RG_BLOCK)
    cols_g = pl.ds(RG_WIDTH + n * RG_BLOCK, RG_BLOCK)
    dst_ref[0, n] = jnp.dot(u, w_ref[:, cols_x], preferred_element_type=F32)
    dst_ref[1, n] = jnp.dot(u, w_ref[:, cols_g], preferred_element_type=F32)


def _rg_back(n, src_ref, obuf_ref, xbuf_ref, carry_ref, stage_ref, cw_ref, cb_ref, wa_ref, ba_ref,
             wx_ref, bx_ref, lam_ref):
    tm = obuf_ref.shape[1]
    group = (SUBLANES, RG_BLOCK)
    row = lax.broadcasted_iota(jnp.int32, group, 0)
    xc = _causal_conv(src_ref[0, n], xbuf_ref.at[n], slice(None), cw_ref[n], cb_ref[n], tm)
    xcb = xc.astype(BF16)
    stage_ref[0, n] = xc
    stage_ref[1, n] = jnp.dot(xcb, wa_ref[n], preferred_element_type=F32)
    stage_ref[2, n] = jnp.dot(xcb, wx_ref[n], preferred_element_type=F32)
    scale = jnp.broadcast_to(-RG_C * _softplus(-lam_ref[n]), group)
    ba = jnp.broadcast_to(ba_ref[n], group)
    bx = jnp.broadcast_to(bx_ref[n], group)

    carry = carry_ref[n]
    out_rows = 2 * SUBLANES
    hs = []
    for r0 in range(0, tm, SUBLANES):
        rows = pl.ds(r0, SUBLANES)
        xg = stage_ref[0, n, rows, :]
        r = jax.nn.sigmoid(stage_ref[1, n, rows, :] + ba)
        i = jax.nn.sigmoid(stage_ref[2, n, rows, :] + bx)
        log_a = r * scale
        a = jnp.exp(log_a)
        v = -jnp.tanh(log_a) * (a * a + 1.0)
        b = jnp.where(v > 0.0, v * lax.rsqrt(v), 0.0) * (i * xg)
        for k in (1, 2, 4):
            keep = row >= k
            a_prev = pltpu.roll(a, k, axis=0)
            b_prev = pltpu.roll(b, k, axis=0)
            b = jnp.where(keep, a * b_prev + b, b)
            a = jnp.where(keep, a * a_prev, a)
        h = b + a * carry
        carry = h[SUBLANES - 1:SUBLANES, :]
        hs.append(h * jax.nn.gelu(src_ref[1, n, rows, :]))
        if len(hs) * SUBLANES == out_rows:
            obuf_ref[n, pl.ds(r0 + SUBLANES - out_rows, out_rows), :] = (
                jnp.concatenate(hs, axis=0).astype(BF16))
            hs = []
    carry_ref[n] = carry


def _rg_mixer_kernel(u_ref, wt_ref, cw_ref, cb_ref, wa_ref, ba_ref, wx_ref, bx_ref, lam_ref,
                     side0_ref, side1_ref, side2_ref,
                     o_ref, cast0_ref, cast1_ref, cast2_ref,
                     p0_ref, p1_ref, xbuf_ref, carry_ref, obuf_ref, stage_ref):
    s = pl.program_id(0)

    for side_ref, cast_ref in ((side0_ref, cast0_ref), (side1_ref, cast1_ref),
                               (side2_ref, cast2_ref)):
        cast_ref[...] = side_ref[...].astype(BF16)

    @pl.when(s == 0)
    def _():
        p1_ref[...] = jnp.zeros_like(p1_ref)

    @pl.when(s <= 1)
    def _():
        xbuf_ref[:, pl.ds(0, SUBLANES), :] = jnp.zeros((RG_BLOCKS, SUBLANES, RG_BLOCK), F32)
        carry_ref[...] = jnp.zeros_like(carry_ref)

    def step(dst_ref, src_ref):
        u = u_ref[...]
        for n in range(RG_BLOCKS):
            _rg_front(n, u, wt_ref, dst_ref)
            _rg_back(n, src_ref, obuf_ref, xbuf_ref, carry_ref, stage_ref, cw_ref, cb_ref, wa_ref,
                     ba_ref, wx_ref, bx_ref, lam_ref)
            o_ref[:, pl.ds(n * RG_BLOCK, RG_BLOCK)] = obuf_ref[n]

    pl.when(s % 2 == 0)(lambda: step(p0_ref, p1_ref))
    pl.when(s % 2 == 1)(lambda: step(p1_ref, p0_ref))


def _rg_mixer(u, w_t, conv_w, conv_b, w_a, b_a, w_x, b_x, lam, side):
    s, d = u.shape
    tm = MIX_TM
    n_tiles = s // tm
    steps, cur_map, prev_map = _skewed_steps(n_tiles)
    blk = RG_BLOCK
    side_specs = [pl.BlockSpec((w.shape[0] // n_tiles, w.shape[1]), cur_map) for w in side]
    return pl.pallas_call(
        _rg_mixer_kernel,
        grid=(steps,),
        in_specs=[
            pl.BlockSpec((tm, d), cur_map),
            _resident((d, 2 * RG_WIDTH), ROW_RG // (2 * RG_WIDTH)),
            _resident((RG_BLOCKS, CONV_WIDTH, blk)),
            _resident((RG_BLOCKS, 1, blk)),
            _resident((RG_BLOCKS, blk, blk)),
            _resident((RG_BLOCKS, 1, blk)),
            _resident((RG_BLOCKS, blk, blk)),
            _resident((RG_BLOCKS, 1, blk)),
            _resident((RG_BLOCKS, 1, blk)),
        ] + side_specs,
        out_specs=[pl.BlockSpec((tm, RG_WIDTH), prev_map)] + side_specs,
        out_shape=([jax.ShapeDtypeStruct((s, RG_WIDTH), BF16)]
                   + [jax.ShapeDtypeStruct(w.shape, BF16) for w in side]),
        scratch_shapes=[
            pltpu.VMEM((2, RG_BLOCKS, tm, blk), F32),
            pltpu.VMEM((2, RG_BLOCKS, tm, blk), F32),
            pltpu.VMEM((RG_BLOCKS, tm + SUBLANES, blk), F32),
            pltpu.VMEM((RG_BLOCKS, 1, blk), F32),
            pltpu.VMEM((RG_BLOCKS, tm, blk), BF16),
            pltpu.VMEM((3, RG_BLOCKS, tm, blk), F32),
        ],
        compiler_params=_compiler_params(("arbitrary",)),
        name="rg_mixer",
    )(u, w_t, conv_w, conv_b, w_a, b_a, w_x, b_x, lam, *side)


def _mlstm_tile_qkv(ux, xbuf_ref, cw_ref, cb_ref, wq_ref, wk_ref, wv_ref):
    tm, D = ux.shape[0], M_HEAD_DIM
    conv = _causal_conv(ux, xbuf_ref, slice(None), cw_ref[...], cb_ref[...], tm)
    xc = conv * jax.nn.sigmoid(conv)
    heads = range(M_HEADS)
    sls = [slice(h * D, (h + 1) * D) for h in heads]
    xcb = [xc[:, sl].astype(BF16) for sl in sls]
    q = [jnp.dot(xcb[h], wq_ref[h], preferred_element_type=F32) for h in heads]
    kk = [jnp.dot(xcb[h], wk_ref[h], preferred_element_type=F32) * (M_HEAD_DIM ** -0.5)
          for h in heads]
    v = [jnp.dot(ux[:, sls[h]].astype(BF16), wv_ref[h], preferred_element_type=F32)
         for h in heads]
    return xc, q, kk, v


def _mlstm_chunk(xc, q, kk, v, mo, gif, o_ref, rows, c_ref, n_ref, m_ref, bif_ref, ng_ref,
                 skip_ref):
    L, D = M_CHUNK, M_HEAD_DIM

    gt = (gif + bif_ref[...]).T
    log_i = gt[0:M_HEADS, :]
    log_f = -_softplus(-gt[M_HEADS:2 * M_HEADS, :])
    lane = lax.broadcasted_iota(jnp.int32, (M_HEADS, L), 1)
    b = log_f
    k = 1
    while k < L:
        b = b + jnp.where(lane >= k, pltpu.roll(b, k, axis=1), 0.0)
        k *= 2
    g = b[:, L - 1:L]
    r = log_i - b
    a = g + r
    m_loc = jnp.max(a, axis=1, keepdims=True)
    w = jnp.exp(a - m_loc)
    m_prev = m_ref[:, 0:1]
    m_new = jnp.maximum(g + m_prev, m_loc)
    s_old = jnp.exp(g + m_prev - m_new)
    s_loc = jnp.exp(m_loc - m_new)
    m_ref[...] = jnp.broadcast_to(m_new, m_ref.shape)

    rmax = r
    k = 1
    while k < L:
        rmax = jnp.maximum(rmax, jnp.where(lane >= k, pltpu.roll(rmax, k, axis=1), -jnp.inf))
        k *= 2
    cm = jnp.maximum(m_prev, rmax)
    iw = jnp.exp(m_prev - cm)
    em = jnp.exp(-(b + cm))

    cols = jnp.concatenate([cm, w, iw, em, jnp.zeros((LANES - 4 * M_HEADS, L), F32)], axis=0).T

    tri = (lax.broadcasted_iota(jnp.int32, (L, L), 0)
           >= lax.broadcasted_iota(jnp.int32, (L, L), 1))

    heads = range(M_HEADS)
    sls = [slice(h * D, (h + 1) * D) for h in heads]

    qb = [x.astype(BF16) for x in q]
    kb = [x.astype(BF16) for x in kk]
    qk = [lax.dot_general(qb[h], kb[h], _NT_DIMS, preferred_element_type=F32) for h in heads]

    col = lambda which, h: cols[:, which * M_HEADS + h:which * M_HEADS + h + 1]
    cm_col = [col(0, h) for h in heads]
    w_col = [col(1, h) for h in heads]
    inter_w = [col(2, h) for h in heads]
    em_col = [col(3, h) for h in heads]
    c_prev = [c_ref[h] for h in heads]
    n_prev = [n_ref[h] for h in heads]
    qc = [jnp.dot(qb[h], c_prev[h].astype(BF16), preferred_element_type=F32) for h in heads]
    c_loc = [jnp.dot(kk[h].T.astype(BF16), (w_col[h] * v[h]).astype(BF16),
                     preferred_element_type=F32) for h in heads]

    scores = [qk[h] * jnp.exp(jnp.where(tri, r[h:h + 1, :] - cm_col[h], -jnp.inf))
              for h in heads]
    sv = [jnp.dot(scores[h].astype(BF16), v[h].astype(BF16), preferred_element_type=F32)
          for h in heads]

    for h in heads:
        s_old_h = s_old[h:h + 1, :]
        s_loc_h = s_loc[h:h + 1, :]
        c_ref[h] = s_old_h * c_prev[h] + s_loc_h * c_loc[h]
        n_loc = jnp.sum(w_col[h] * kk[h], axis=0, keepdims=True)
        n_ref[h] = s_old_h * n_prev[h] + s_loc_h * n_loc

    for h in heads:
        sl = sls[h]
        num = sv[h] + inter_w[h] * qc[h]
        den = (jnp.sum(scores[h], axis=1, keepdims=True)
               + inter_w[h] * jnp.sum(q[h] * n_prev[h], axis=1, keepdims=True))
        hh = num * (1.0 / jnp.maximum(jnp.abs(den), em_col[h]))
        mu = jnp.mean(hh, axis=1, keepdims=True)
        cen = hh - mu
        var = jnp.mean(cen * cen, axis=1, keepdims=True)
        y = cen * lax.rsqrt(var + NORM_EPS) * ng_ref[:, sl] + skip_ref[:, sl] * xc[:, sl]
        o_ref[rows, sl] = (jax.nn.sigmoid(mo[:, sl]) * y).astype(o_ref.dtype)


def _m_mixer_kernel(u_ref, wt_ref, wift_ref, cw_ref, cb_ref, wq_ref, wk_ref, wv_ref, bif_ref,
                    ng_ref, skip_ref, side0_ref, side1_ref, side2_ref,
                    o_ref, cast0_ref, cast1_ref, cast2_ref,
                    p0_ref, p1_ref, g0_ref, g1_ref, xbuf_ref, c_ref, n_ref, m_ref):
    s = pl.program_id(0)

    for side_ref, cast_ref in ((side0_ref, cast0_ref), (side1_ref, cast1_ref),
                               (side2_ref, cast2_ref)):
        cast_ref[...] = side_ref[...].astype(BF16)

    @pl.when(s == 0)
    def _():
        p1_ref[...] = jnp.zeros_like(p1_ref)
        g1_ref[...] = jnp.zeros_like(g1_ref)

    @pl.when(s <= 1)
    def _():
        xbuf_ref[pl.ds(0, SUBLANES), :] = jnp.zeros((SUBLANES, M_WIDTH), F32)
        c_ref[...] = jnp.zeros_like(c_ref)
        n_ref[...] = jnp.zeros_like(n_ref)
        m_ref[...] = jnp.zeros_like(m_ref)

    def step(dst_ref, gdst_ref, src_ref, gsrc_ref):
        u = u_ref[...]
        for c0 in range(0, 2 * M_WIDTH, M_PROJ_PIECE):
            cols = pl.ds(c0, M_PROJ_PIECE)
            dst_ref[:, cols] = jnp.dot(u, wt_ref[:, cols], preferred_element_type=F32)
        gdst_ref[...] = lax.dot_general(u, wift_ref[...], _NT_DIMS, preferred_element_type=F32)

        xc, q, kk, v = _mlstm_tile_qkv(src_ref[:, pl.ds(0, M_WIDTH)], xbuf_ref, cw_ref, cb_ref,
                                       wq_ref, wk_ref, wv_ref)
        for r0 in range(0, o_ref.shape[0], M_CHUNK):
            rows = pl.ds(r0, M_CHUNK)
            part = lambda xs: [x[r0:r0 + M_CHUNK, :] for x in xs]
            _mlstm_chunk(xc[r0:r0 + M_CHUNK, :], part(q), part(kk), part(v),
                         src_ref[rows, pl.ds(M_WIDTH, M_WIDTH)], gsrc_ref[rows, :], o_ref, rows,
                         c_ref, n_ref, m_ref, bif_ref, ng_ref, skip_ref)

    pl.when(s % 2 == 0)(lambda: step(p0_ref, g0_ref, p1_ref, g1_ref))
    pl.when(s % 2 == 1)(lambda: step(p1_ref, g1_ref, p0_ref, g0_ref))


def _m_mixer(u, w_t, w_if_t, conv_w, conv_b, w_q, w_k, w_v, b_if, norm_g, skip, side):
    s, d = u.shape
    tm = MIX_TM
    n_tiles = s // tm
    steps, cur_map, prev_map = _skewed_steps(n_tiles)
    head_w = (M_HEADS, M_HEAD_DIM, M_HEAD_DIM)
    side_specs = [pl.BlockSpec((w.shape[0] // n_tiles, w.shape[1]), cur_map) for w in side]
    return pl.pallas_call(
        _m_mixer_kernel,
        grid=(steps,),
        in_specs=[
            pl.BlockSpec((tm, d), cur_map),
            _resident((d, 2 * M_WIDTH), ROW_M // (2 * M_WIDTH)),
            _resident((LANES, d)),
            _resident((CONV_WIDTH, M_WIDTH)),
            _resident((1, M_WIDTH)),
            _resident(head_w),
            _resident(head_w),
            _resident(head_w),
            _resident((1, LANES)),
            _resident((1, M_WIDTH)),
            _resident((1, M_WIDTH)),
        ] + side_specs,
        out_specs=[pl.BlockSpec((tm, M_WIDTH), prev_map)] + side_specs,
        out_shape=([jax.ShapeDtypeStruct((s, M_WIDTH), BF16)]
                   + [jax.ShapeDtypeStruct(w.shape, BF16) for w in side]),
        scratch_shapes=[
            pltpu.VMEM((tm, 2 * M_WIDTH), F32),
            pltpu.VMEM((tm, 2 * M_WIDTH), F32),
            pltpu.VMEM((tm, LANES), F32),
            pltpu.VMEM((tm, LANES), F32),
            pltpu.VMEM((tm + SUBLANES, M_WIDTH), F32),
            pltpu.VMEM(head_w, F32),
            pltpu.VMEM((M_HEADS, 1, M_HEAD_DIM), F32),
            pltpu.VMEM((M_HEADS, LANES), F32),
        ],
        compiler_params=_compiler_params(("arbitrary",)),
        name="m_mixer",
    )(u, w_t, w_if_t, conv_w, conv_b, w_q, w_k, w_v, b_if, norm_g, skip, *side)


def _merge_kernel(yrg_ref, ym_ref, grg_ref, gm_ref, wrg_ref, wm_ref, wout_ref, h_ref, o_ref):
    merged = (jax.nn.sigmoid(grg_ref[...])
              * jnp.dot(yrg_ref[...], wrg_ref[...], preferred_element_type=F32)
              + jax.nn.sigmoid(gm_ref[...])
              * jnp.dot(ym_ref[...], wm_ref[...], preferred_element_type=F32))
    o_ref[...] = h_ref[...] + jnp.dot(merged.astype(BF16), wout_ref[...],
                                      preferred_element_type=F32)


def _merge(y_rg, y_m, gates, w_rg, w_m, w_out, h1):
    s, d = h1.shape
    tm = MERGE_TM
    return pl.pallas_call(
        _merge_kernel,
        grid=(s // tm,),
        in_specs=[
            pl.BlockSpec((tm, RG_WIDTH), lambda i: (i, 0)),
            pl.BlockSpec((tm, M_WIDTH), lambda i: (i, 0)),
            pl.BlockSpec((tm, d), lambda i: (i, 0)),
            pl.BlockSpec((tm, d), lambda i: (i, 1)),
            _resident((RG_WIDTH, d)),
            _resident((M_WIDTH, d)),
            _resident((d, d)),
            pl.BlockSpec((tm, d), lambda i: (i, 0)),
        ],
        out_specs=pl.BlockSpec((tm, d), lambda i: (i, 0)),
        out_shape=jax.ShapeDtypeStruct((s, d), F32),
        compiler_params=_compiler_params(("parallel",)),
        name="merge",
    )(y_rg, y_m, gates, gates, w_rg, w_m, w_out, h1)


def kernel(x, ffn1_norm, ffn1_w_gate, ffn1_w_up, ffn1_w_down, mix_norm, w_in, rg_conv_w, rg_conv_b, rg_w_a, rg_b_a, rg_w_x, rg_b_x, rg_lambda, m_conv_w, m_conv_b, m_w_q, m_w_k, m_w_v, m_b_i, m_b_f, m_norm, m_skip, w_proj_rg, w_proj_m, w_out, ffn2_norm, ffn2_w_gate, ffn2_w_up, ffn2_w_down, final_norm):
    bsz, s, d = x.shape
    assert (bsz, s, d) == (1, SEQ, D_MODEL) and ffn1_norm.shape[0] == 1
    bf = lambda w: w.astype(BF16)
    h = x[0]

    h, u_mix = _ffn(h, ffn1_norm, ffn1_w_gate[0], ffn1_w_up[0], ffn1_w_down[0], mix_norm,
                    final_norm=False)

    w_in_t = jnp.swapaxes(w_in[0], 0, 1)
    w_if_t = jnp.pad(w_in_t[ROW_IF:], ((0, LANES - 2 * M_HEADS), (0, 0)))
    gates, w_mix = _gate_proj(u_mix, w_in_t)

    per_block = lambda v: v.reshape(-1, RG_BLOCKS, RG_BLOCK).swapaxes(0, 1)
    y_rg, w2_gate, w2_up, w2_down = _rg_mixer(
        u_mix, w_mix, per_block(rg_conv_w[0]), per_block(rg_conv_b),
        bf(rg_w_a[0]), rg_b_a[0][:, None, :], bf(rg_w_x[0]), rg_b_x[0][:, None, :],
        per_block(rg_lambda), side=(ffn2_w_gate[0], ffn2_w_up[0], ffn2_w_down[0]))

    b_if = jnp.pad(jnp.concatenate([m_b_i, m_b_f], axis=1), ((0, 0), (0, LANES - 2 * M_HEADS)))
    y_m, wp_rg, wp_m, wp_out = _m_mixer(
        u_mix, w_mix, bf(w_if_t), m_conv_w[0], m_conv_b, bf(m_w_q[0]), bf(m_w_k[0]),
        bf(m_w_v[0]), b_if, m_norm, m_skip, side=(w_proj_rg[0], w_proj_m[0], w_out[0]))

    h = _merge(y_rg, y_m, gates, wp_rg, wp_m, wp_out, h)

    h = _ffn(h, ffn2_norm, w2_gate, w2_up, w2_down, final_norm[None, :], final_norm=True)
    return h[None]
```

```python
import functools

import jax
import jax.numpy as jnp
from jax import lax
from jax.experimental import pallas as pl
from jax.experimental.pallas import tpu as pltpu

D_MODEL = 2048
SEQ = 8192
D_FF = 5632
CONV_WIDTH = 4
RG_WIDTH = 2048
RG_BLOCKS = 8
RG_BLOCK = RG_WIDTH // RG_BLOCKS
RG_C = 8.0
M_HEADS = 8
M_HEAD_DIM = 256
M_WIDTH = M_HEADS * M_HEAD_DIM
M_CHUNK = 128
NORM_EPS = 1e-6

LANES = 128
SUBLANES = 8
V7X_VMEM_BYTES = 64 * 1024 * 1024
VMEM_LIMIT_BYTES = (V7X_VMEM_BYTES // 16) * 15
VMEM_LIMIT_WIDE_BYTES = (V7X_VMEM_BYTES // 64) * 63

ROW_RG = 0
ROW_M = 2 * RG_WIDTH
ROW_GATES = ROW_M + 2 * M_WIDTH
ROW_IF = ROW_GATES + 2 * D_MODEL

FFN_TM, FFN_TF = 1024, 256
FFN_TF_BF16 = 512
FFN_ROW_CHUNK = 128
GATE_TM, GATE_TN = 2048, 512
MIX_TM = 256
M_PROJ_PIECE = 512
MERGE_TM = 256

BF16 = jnp.bfloat16
F32 = jnp.float32
_NT_DIMS = (((1,), (1,)), ((), ()))


def _rms_norm(x, g):
    ms = jnp.mean(x * x, axis=-1, keepdims=True)
    return x * lax.rsqrt(ms + NORM_EPS) * g


def _softplus(z):
    return jnp.maximum(z, 0.0) + jnp.log1p(jnp.exp(-jnp.abs(z)))


def _compiler_params(semantics, vmem_limit=None):
    return pltpu.CompilerParams(dimension_semantics=semantics,
                                vmem_limit_bytes=vmem_limit or VMEM_LIMIT_BYTES)


def _resident(shape, col_block=0):
    index = (0,) * (len(shape) - 1) + (col_block,)
    return pl.BlockSpec(shape, lambda *_: index, pipeline_mode=pl.Buffered(1))


def _ffn_kernel(*refs, final_norm, tf):
    x_ref, g_ref, wg_hbm, wu_hbm, wd_hbm, eg_ref = refs[:6]
    if final_norm:
        o_ref, u_ref, wg_buf, wu_buf, wd_buf, sem = refs[6:]
    else:
        o_ref, un_ref, u_ref, wg_buf, wu_buf, wd_buf, sem = refs[6:]
    i = pl.program_id(0)
    nj = wg_hbm.shape[1] // tf
    n_tiles = pl.num_programs(0) * nj

    def tile_copies(j, slot):
        cols = pl.ds(pl.multiple_of(j * tf, tf), tf)
        return (pltpu.make_async_copy(wg_hbm.at[:, cols], wg_buf.at[slot], sem.at[0, slot]),
                pltpu.make_async_copy(wu_hbm.at[:, cols], wu_buf.at[slot], sem.at[1, slot]),
                pltpu.make_async_copy(wd_hbm.at[cols, :], wd_buf.at[slot], sem.at[2, slot]))

    @pl.when(i == 0)
    def _():
        for copy in tile_copies(0, 0):
            copy.start()

    row_chunks = [pl.ds(r, FFN_ROW_CHUNK) for r in range(0, x_ref.shape[0], FFN_ROW_CHUNK)]
    for rows in row_chunks:
        u_ref[rows, :] = _rms_norm(x_ref[rows, :], g_ref[...]).astype(BF16)
    o_ref[...] = jnp.zeros_like(o_ref)

    def hidden_tile(j, carry):
        t = i * nj + j
        slot = lax.rem(t, 2)

        @pl.when(t + 1 < n_tiles)
        def _():
            for copy in tile_copies(lax.rem(j + 1, nj), 1 - slot):
                copy.start()

        for copy in tile_copies(j, slot):
            copy.wait()
        u = u_ref[...]
        gate = jnp.dot(u, wg_buf[slot].astype(BF16), preferred_element_type=F32)
        up = jnp.dot(u, wu_buf[slot].astype(BF16), preferred_element_type=F32)
        act = (gate * jax.nn.sigmoid(gate) * up).astype(BF16)
        o_ref[...] += jnp.dot(act, wd_buf[slot].astype(BF16), preferred_element_type=F32)
        return carry

    lax.fori_loop(0, nj, hidden_tile, 0)

    for rows in row_chunks:
        h = x_ref[rows, :] + 0.5 * o_ref[rows, :]
        if final_norm:
            o_ref[rows, :] = _rms_norm(h, eg_ref[...])
        else:
            o_ref[rows, :] = h
            un_ref[rows, :] = _rms_norm(h, eg_ref[...]).astype(BF16)


def _ffn(x, norm_g, wg, wu, wd, extra_g, final_norm):
    s, d = x.shape
    f = wg.shape[1]
    tm = FFN_TM
    bf16_weights = wg.dtype == BF16
    tf = FFN_TF_BF16 if bf16_weights else FFN_TF
    vmem_limit = VMEM_LIMIT_WIDE_BYTES if bf16_weights else VMEM_LIMIT_BYTES
    assert f % tf == 0 and s % tm == 0
    row_spec = pl.BlockSpec((tm, d), lambda i: (i, 0))
    gain_spec = pl.BlockSpec((1, d), lambda i: (0, 0))
    hbm_spec = pl.BlockSpec(memory_space=pl.ANY)
    out_specs, out_shape = row_spec, jax.ShapeDtypeStruct((s, d), F32)
    if not final_norm:
        out_specs = [row_spec, row_spec]
        out_shape = [out_shape, jax.ShapeDtypeStruct((s, d), BF16)]
    return pl.pallas_call(
        functools.partial(_ffn_kernel, final_norm=final_norm, tf=tf),
        grid=(s // tm,),
        in_specs=[row_spec, gain_spec, hbm_spec, hbm_spec, hbm_spec, gain_spec],
        out_specs=out_specs,
        out_shape=out_shape,
        scratch_shapes=[
            pltpu.VMEM((tm, d), BF16),
            pltpu.VMEM((2, d, tf), wg.dtype),
            pltpu.VMEM((2, d, tf), wu.dtype),
            pltpu.VMEM((2, tf, d), wd.dtype),
            pltpu.SemaphoreType.DMA((3, 2)),
        ],
        compiler_params=_compiler_params(("arbitrary",), vmem_limit),
        name="ffn_final" if final_norm else "ffn",
    )(x, norm_g, wg, wu, wd, extra_g)


def _gate_proj_kernel(u_ref, wt_ref, side_ref, o_ref, cast_ref):
    o_ref[...] = lax.dot_general(u_ref[...], wt_ref[...].astype(BF16), _NT_DIMS,
                                 preferred_element_type=F32)
    cast_ref[...] = side_ref[...].T.astype(BF16)


def _gate_proj(u, w_in_t):
    s, d = u.shape
    n = 2 * D_MODEL
    tm, tn = GATE_TM, GATE_TN
    row0 = ROW_GATES // tn
    nj = n // tn
    side_rows = ROW_GATES // ((s // tm) * nj)
    return pl.pallas_call(
        _gate_proj_kernel,
        grid=(s // tm, nj),
        in_specs=[
            pl.BlockSpec((tm, d), lambda i, j: (i, 0)),
            pl.BlockSpec((tn, d), lambda i, j: (row0 + j, 0)),
            pl.BlockSpec((side_rows, d), lambda i, j: (i * nj + j, 0)),
        ],
        out_specs=[
            pl.BlockSpec((tm, tn), lambda i, j: (i, j)),
            pl.BlockSpec((d, side_rows), lambda i, j: (0, i * nj + j)),
        ],
        out_shape=[jax.ShapeDtypeStruct((s, n), F32),
                   jax.ShapeDtypeStruct((d, ROW_GATES), BF16)],
        compiler_params=_compiler_params(("arbitrary", "arbitrary")),
        name="gate_proj",
    )(u, w_in_t, w_in_t)


def _causal_conv(x, xbuf_ref, cols, cw, cb, rows):
    xbuf_ref[pl.ds(SUBLANES, rows), cols] = x
    out = cb + cw[CONV_WIDTH - 1:CONV_WIDTH, :] * x
    for back in range(1, CONV_WIDTH):
        tap = CONV_WIDTH - 1 - back
        out = out + cw[tap:tap + 1, :] * xbuf_ref[pl.ds(SUBLANES - back, rows), cols]
    xbuf_ref[pl.ds(0, SUBLANES), cols] = xbuf_ref[pl.ds(rows, SUBLANES), cols]
    return out


def _skewed_steps(n_tiles):
    return (n_tiles + 1,
            lambda s: (jnp.minimum(s, n_tiles - 1), 0),
            lambda s: (jnp.maximum(s - 1, 0), 0))


def _rg_front(n, u, w_ref, dst_ref):
    cols_x = pl.ds(n * RG_BLOCK, RG_BLOCK)
    cols_g = pl.ds(RG_WIDTH + n * RG_BLOCK, RG_BLOCK)
    dst_ref[0, n] = jnp.dot(u, w_ref[:, cols_x], preferred_element_type=F32)
    dst_ref[1, n] = jnp.dot(u, w_ref[:, cols_g], preferred_element_type=F32)


def _rg_back(n, src_ref, obuf_ref, xbuf_ref, carry_ref, stage_ref, cw_ref, cb_ref, wa_ref, ba_ref,
             wx_ref, bx_ref, lam_ref):
    tm = obuf_ref.shape[1]
    group = (SUBLANES, RG_BLOCK)
    row = lax.broadcasted_iota(jnp.int32, group, 0)
    xc = _causal_conv(src_ref[0, n], xbuf_ref.at[n], slice(None), cw_ref[n], cb_ref[n], tm)
    xcb = xc.astype(BF16)
    stage_ref[0, n] = xc
    stage_ref[1, n] = jnp.dot(xcb, wa_ref[n], preferred_element_type=F32)
    stage_ref[2, n] = jnp.dot(xcb, wx_ref[n], preferred_element_type=F32)
    scale = jnp.broadcast_to(-RG_C * _softplus(-lam_ref[n]), group)
    ba = jnp.broadcast_to(ba_ref[n], group)
    bx = jnp.broadcast_to(bx_ref[n], group)

    carry = carry_ref[n]
    out_rows = 2 * SUBLANES
    hs = []
    for r0 in range(0, tm, SUBLANES):
        rows = pl.ds(r0, SUBLANES)
        xg = stage_ref[0, n, rows, :]
        r = jax.nn.sigmoid(stage_ref[1, n, rows, :] + ba)
        i = jax.nn.sigmoid(stage_ref[2, n, rows, :] + bx)
        log_a = r * scale
        a = jnp.exp(log_a)
        v = -jnp.tanh(log_a) * (a * a + 1.0)
        b = jnp.where(v > 0.0, v * lax.rsqrt(v), 0.0) * (i * xg)
        for k in (1, 2, 4):
            keep = row >= k
            a_prev = pltpu.roll(a, k, axis=0)
            b_prev = pltpu.roll(b, k, axis=0)
            b = jnp.where(keep, a * b_prev + b, b)
            a = jnp.where(keep, a * a_prev, a)
        h = b + a * carry
        carry = h[SUBLANES - 1:SUBLANES, :]
        hs.append(h * jax.nn.gelu(src_ref[1, n, rows, :]))
        if len(hs) * SUBLANES == out_rows:
            obuf_ref[n, pl.ds(r0 + SUBLANES - out_rows, out_rows), :] = (
                jnp.concatenate(hs, axis=0).astype(BF16))
            hs = []
    carry_ref[n] = carry


def _rg_mixer_kernel(u_ref, wt_ref, cw_ref, cb_ref, wa_ref, ba_ref, wx_ref, bx_ref, lam_ref,
                     side0_ref, side1_ref, side2_ref,
                     o_ref, cast0_ref, cast1_ref, cast2_ref,
                     p0_ref, p1_ref, xbuf_ref, carry_ref, obuf_ref, stage_ref):
    s = pl.program_id(0)

    for side_ref, cast_ref in ((side0_ref, cast0_ref), (side1_ref, cast1_ref),
                               (side2_ref, cast2_ref)):
        cast_ref[...] = side_ref[...].astype(BF16)

    @pl.when(s == 0)
    def _():
        p1_ref[...] = jnp.zeros_like(p1_ref)

    @pl.when(s <= 1)
    def _():
        xbuf_ref[:, pl.ds(0, SUBLANES), :] = jnp.zeros((RG_BLOCKS, SUBLANES, RG_BLOCK), F32)
        carry_ref[...] = jnp.zeros_like(carry_ref)

    def step(dst_ref, src_ref):
        u = u_ref[...]
        for n in range(RG_BLOCKS):
            _rg_front(n, u, wt_ref, dst_ref)
            _rg_back(n, src_ref, obuf_ref, xbuf_ref, carry_ref, stage_ref, cw_ref, cb_ref, wa_ref,
                     ba_ref, wx_ref, bx_ref, lam_ref)
            o_ref[:, pl.ds(n * RG_BLOCK, RG_BLOCK)] = obuf_ref[n]

    pl.when(s % 2 == 0)(lambda: step(p0_ref, p1_ref))
    pl.when(s % 2 == 1)(lambda: step(p1_ref, p0_ref))


def _rg_mixer(u, w_t, conv_w, conv_b, w_a, b_a, w_x, b_x, lam, side):
    s, d = u.shape
    tm = MIX_TM
    n_tiles = s // tm
    steps, cur_map, prev_map = _skewed_steps(n_tiles)
    blk = RG_BLOCK
    side_specs = [pl.BlockSpec((w.shape[0] // n_tiles, w.shape[1]), cur_map) for w in side]
    return pl.pallas_call(
        _rg_mixer_kernel,
        grid=(steps,),
        in_specs=[
            pl.BlockSpec((tm, d), cur_map),
            _resident((d, 2 * RG_WIDTH), ROW_RG // (2 * RG_WIDTH)),
            _resident((RG_BLOCKS, CONV_WIDTH, blk)),
            _resident((RG_BLOCKS, 1, blk)),
            _resident((RG_BLOCKS, blk, blk)),
            _resident((RG_BLOCKS, 1, blk)),
            _resident((RG_BLOCKS, blk, blk)),
            _resident((RG_BLOCKS, 1, blk)),
            _resident((RG_BLOCKS, 1, blk)),
        ] + side_specs,
        out_specs=[pl.BlockSpec((tm, RG_WIDTH), prev_map)] + side_specs,
        out_shape=([jax.ShapeDtypeStruct((s, RG_WIDTH), BF16)]
                   + [jax.ShapeDtypeStruct(w.shape, BF16) for w in side]),
        scratch_shapes=[
            pltpu.VMEM((2, RG_BLOCKS, tm, blk), F32),
            pltpu.VMEM((2, RG_BLOCKS, tm, blk), F32),
            pltpu.VMEM((RG_BLOCKS, tm + SUBLANES, blk), F32),
            pltpu.VMEM((RG_BLOCKS, 1, blk), F32),
            pltpu.VMEM((RG_BLOCKS, tm, blk), BF16),
            pltpu.VMEM((3, RG_BLOCKS, tm, blk), F32),
        ],
        compiler_params=_compiler_params(("arbitrary",)),
        name="rg_mixer",
    )(u, w_t, conv_w, conv_b, w_a, b_a, w_x, b_x, lam, *side)


def _mlstm_tile_qkv(ux, xbuf_ref, cw_ref, cb_ref, wq_ref, wk_ref, wv_ref):
    tm, D = ux.shape[0], M_HEAD_DIM
    conv = _causal_conv(ux, xbuf_ref, slice(None), cw_ref[...], cb_ref[...], tm)
    xc = conv * jax.nn.sigmoid(conv)
    heads = range(M_HEADS)
    sls = [slice(h * D, (h + 1) * D) for h in heads]
    xcb = [xc[:, sl].astype(BF16) for sl in sls]
    q = [jnp.dot(xcb[h], wq_ref[h], preferred_element_type=F32) for h in heads]
    kk = [jnp.dot(xcb[h], wk_ref[h], preferred_element_type=F32) * (M_HEAD_DIM ** -0.5)
          for h in heads]
    v = [jnp.dot(ux[:, sls[h]].astype(BF16), wv_ref[h], preferred_element_type=F32)
         for h in heads]
    return xc, q, kk, v


def _mlstm_chunk(xc, q, kk, v, mo, gif, o_ref, rows, c_ref, n_ref, m_ref, bif_ref, ng_ref,
                 skip_ref):
    L, D = M_CHUNK, M_HEAD_DIM

    gt = (gif + bif_ref[...]).T
    log_i = gt[0:M_HEADS, :]
    log_f = -_softplus(-gt[M_HEADS:2 * M_HEADS, :])
    lane = lax.broadcasted_iota(jnp.int32, (M_HEADS, L), 1)
    b = log_f
    k = 1
    while k < L:
        b = b + jnp.where(lane >= k, pltpu.roll(b, k, axis=1), 0.0)
        k *= 2
    g = b[:, L - 1:L]
    r = log_i - b
    a = g + r
    m_loc = jnp.max(a, axis=1, keepdims=True)
    w = jnp.exp(a - m_loc)
    m_prev = m_ref[:, 0:1]
    m_new = jnp.maximum(g + m_prev, m_loc)
    s_old = jnp.exp(g + m_prev - m_new)
    s_loc = jnp.exp(m_loc - m_new)
    m_ref[...] = jnp.broadcast_to(m_new, m_ref.shape)

    rmax = r
    k = 1
    while k < L:
        rmax = jnp.maximum(rmax, jnp.where(lane >= k, pltpu.roll(rmax, k, axis=1), -jnp.inf))
        k *= 2
    cm = jnp.maximum(m_prev, rmax)
    iw = jnp.exp(m_prev - cm)
    em = jnp.exp(-(b + cm))

    cols = jnp.concatenate([cm, w, iw, em, jnp.zeros((LANES - 4 * M_HEADS, L), F32)], axis=0).T

    tri = (lax.broadcasted_iota(jnp.int32, (L, L), 0)
           >= lax.broadcasted_iota(jnp.int32, (L, L), 1))

    heads = range(M_HEADS)
    sls = [slice(h * D, (h + 1) * D) for h in heads]

    qb = [x.astype(BF16) for x in q]
    kb = [x.astype(BF16) for x in kk]
    qk = [lax.dot_general(qb[h], kb[h], _NT_DIMS, preferred_element_type=F32) for h in heads]

    col = lambda which, h: cols[:, which * M_HEADS + h:which * M_HEADS + h + 1]
    cm_col = [col(0, h) for h in heads]
    w_col = [col(1, h) for h in heads]
    inter_w = [col(2, h) for h in heads]
    em_col = [col(3, h) for h in heads]
    c_prev = [c_ref[h] for h in heads]
    n_prev = [n_ref[h] for h in heads]
    qc = [jnp.dot(qb[h], c_prev[h].astype(BF16), preferred_element_type=F32) for h in heads]
    c_loc = [jnp.dot(kk[h].T.astype(BF16), (w_col[h] * v[h]).astype(BF16),
                     preferred_element_type=F32) for h in heads]

    scores = [qk[h] * jnp.exp(jnp.where(tri, r[h:h + 1, :] - cm_col[h], -jnp.inf))
              for h in heads]
    sv = [jnp.dot(scores[h].astype(BF16), v[h].astype(BF16), preferred_element_type=F32)
          for h in heads]

    for h in heads:
        s_old_h = s_old[h:h + 1, :]
        s_loc_h = s_loc[h:h + 1, :]
        c_ref[h] = s_old_h * c_prev[h] + s_loc_h * c_loc[h]
        n_loc = jnp.sum(w_col[h] * kk[h], axis=0, keepdims=True)
        n_ref[h] = s_old_h * n_prev[h] + s_loc_h * n_loc

    for h in heads:
        sl = sls[h]
        num = sv[h] + inter_w[h] * qc[h]
        den = (jnp.sum(scores[h], axis=1, keepdims=True)
               + inter_w[h] * jnp.sum(q[h] * n_prev[h], axis=1, keepdims=True))
        hh = num * (1.0 / jnp.maximum(jnp.abs(den), em_col[h]))
        mu = jnp.mean(hh, axis=1, keepdims=True)
        cen = hh - mu
        var = jnp.mean(cen * cen, axis=1, keepdims=True)
        y = cen * lax.rsqrt(var + NORM_EPS) * ng_ref[:, sl] + skip_ref[:, sl] * xc[:, sl]
        o_ref[rows, sl] = (jax.nn.sigmoid(mo[:, sl]) * y).astype(o_ref.dtype)


def _m_mixer_kernel(u_ref, wt_ref, wift_ref, cw_ref, cb_ref, wq_ref, wk_ref, wv_ref, bif_ref,
                    ng_ref, skip_ref, side0_ref, side1_ref, side2_ref,
                    o_ref, cast0_ref, cast1_ref, cast2_ref,
                    p0_ref, p1_ref, g0_ref, g1_ref, xbuf_ref, c_ref, n_ref, m_ref):
    s = pl.program_id(0)

    for side_ref, cast_ref in ((side0_ref, cast0_ref), (side1_ref, cast1_ref),
                               (side2_ref, cast2_ref)):
        cast_ref[...] = side_ref[...].astype(BF16)

    @pl.when(s == 0)
    def _():
        p1_ref[...] = jnp.zeros_like(p1_ref)
        g1_ref[...] = jnp.zeros_like(g1_ref)

    @pl.when(s <= 1)
    def _():
        xbuf_ref[pl.ds(0, SUBLANES), :] = jnp.zeros((SUBLANES, M_WIDTH), F32)
        c_ref[...] = jnp.zeros_like(c_ref)
        n_ref[...] = jnp.zeros_like(n_ref)
        m_ref[...] = jnp.zeros_like(m_ref)

    def step(dst_ref, gdst_ref, src_ref, gsrc_ref):
        u = u_ref[...]
        for c0 in range(0, 2 * M_WIDTH, M_PROJ_PIECE):
            cols = pl.ds(c0, M_PROJ_PIECE)
            dst_ref[:, cols] = jnp.dot(u, wt_ref[:, cols], preferred_element_type=F32)
        gdst_ref[...] = lax.dot_general(u, wift_ref[...], _NT_DIMS, preferred_element_type=F32)

        xc, q, kk, v = _mlstm_tile_qkv(src_ref[:, pl.ds(0, M_WIDTH)], xbuf_ref, cw_ref, cb_ref,
                                       wq_ref, wk_ref, wv_ref)
        for r0 in range(0, o_ref.shape[0], M_CHUNK):
            rows = pl.ds(r0, M_CHUNK)
            part = lambda xs: [x[r0:r0 + M_CHUNK, :] for x in xs]
            _mlstm_chunk(xc[r0:r0 + M_CHUNK, :], part(q), part(kk), part(v),
                         src_ref[rows, pl.ds(M_WIDTH, M_WIDTH)], gsrc_ref[rows, :], o_ref, rows,
                         c_ref, n_ref, m_ref, bif_ref, ng_ref, skip_ref)

    pl.when(s % 2 == 0)(lambda: step(p0_ref, g0_ref, p1_ref, g1_ref))
    pl.when(s % 2 == 1)(lambda: step(p1_ref, g1_ref, p0_ref, g0_ref))


def _m_mixer(u, w_t, w_if_t, conv_w, conv_b, w_q, w_k, w_v, b_if, norm_g, skip, side):
    s, d = u.shape
    tm = MIX_TM
    n_tiles = s // tm
    steps, cur_map, prev_map = _skewed_steps(n_tiles)
    head_w = (M_HEADS, M_HEAD_DIM, M_HEAD_DIM)
    side_specs = [pl.BlockSpec((w.shape[0] // n_tiles, w.shape[1]), cur_map) for w in side]
    return pl.pallas_call(
        _m_mixer_kernel,
        grid=(steps,),
        in_specs=[
            pl.BlockSpec((tm, d), cur_map),
            _resident((d, 2 * M_WIDTH), ROW_M // (2 * M_WIDTH)),
            _resident((LANES, d)),
            _resident((CONV_WIDTH, M_WIDTH)),
            _resident((1, M_WIDTH)),
            _resident(head_w),
            _resident(head_w),
            _resident(head_w),
            _resident((1, LANES)),
            _resident((1, M_WIDTH)),
            _resident((1, M_WIDTH)),
        ] + side_specs,
        out_specs=[pl.BlockSpec((tm, M_WIDTH), prev_map)] + side_specs,
        out_shape=([jax.ShapeDtypeStruct((s, M_WIDTH), BF16)]
                   + [jax.ShapeDtypeStruct(w.shape, BF16) for w in side]),
        scratch_shapes=[
            pltpu.VMEM((tm, 2 * M_WIDTH), F32),
            pltpu.VMEM((tm, 2 * M_WIDTH), F32),
            pltpu.VMEM((tm, LANES), F32),
            pltpu.VMEM((tm, LANES), F32),
            pltpu.VMEM((tm + SUBLANES, M_WIDTH), F32),
            pltpu.VMEM(head_w, F32),
            pltpu.VMEM((M_HEADS, 1, M_HEAD_DIM), F32),
            pltpu.VMEM((M_HEADS, LANES), F32),
        ],
        compiler_params=_compiler_params(("arbitrary",)),
        name="m_mixer",
    )(u, w_t, w_if_t, conv_w, conv_b, w_q, w_k, w_v, b_if, norm_g, skip, *side)


def _merge_kernel(yrg_ref, ym_ref, grg_ref, gm_ref, wrg_ref, wm_ref, wout_ref, h_ref, o_ref):
    merged = (jax.nn.sigmoid(grg_ref[...])
              * jnp.dot(yrg_ref[...], wrg_ref[...], preferred_element_type=F32)
              + jax.nn.sigmoid(gm_ref[...])
              * jnp.dot(ym_ref[...], wm_ref[...], preferred_element_type=F32))
    o_ref[...] = h_ref[...] + jnp.dot(merged.astype(BF16), wout_ref[...],
                                      preferred_element_type=F32)


def _merge(y_rg, y_m, gates, w_rg, w_m, w_out, h1):
    s, d = h1.shape
    tm = MERGE_TM
    return pl.pallas_call(
        _merge_kernel,
        grid=(s // tm,),
        in_specs=[
            pl.BlockSpec((tm, RG_WIDTH), lambda i: (i, 0)),
            pl.BlockSpec((tm, M_WIDTH), lambda i: (i, 0)),
            pl.BlockSpec((tm, d), lambda i: (i, 0)),
            pl.BlockSpec((tm, d), lambda i: (i, 1)),
            _resident((RG_WIDTH, d)),
            _resident((M_WIDTH, d)),
            _resident((d, d)),
            pl.BlockSpec((tm, d), lambda i: (i, 0)),
        ],
        out_specs=pl.BlockSpec((tm, d), lambda i: (i, 0)),
        out_shape=jax.ShapeDtypeStruct((s, d), F32),
        compiler_params=_compiler_params(("parallel",)),
        name="merge",
    )(y_rg, y_m, gates, gates, w_rg, w_m, w_out, h1)


def kernel(x, ffn1_norm, ffn1_w_gate, ffn1_w_up, ffn1_w_down, mix_norm, w_in, rg_conv_w, rg_conv_b, rg_w_a, rg_b_a, rg_w_x, rg_b_x, rg_lambda, m_conv_w, m_conv_b, m_w_q, m_w_k, m_w_v, m_b_i, m_b_f, m_norm, m_skip, w_proj_rg, w_proj_m, w_out, ffn2_norm, ffn2_w_gate, ffn2_w_up, ffn2_w_down, final_norm):
    bsz, s, d = x.shape
    assert (bsz, s, d) == (1, SEQ, D_MODEL) and ffn1_norm.shape[0] == 1
    bf = lambda w: w.astype(BF16)
    h = x[0]

    h, u_mix = _ffn(h, ffn1_norm, ffn1_w_gate[0], ffn1_w_up[0], ffn1_w_down[0], mix_norm,
                    final_norm=False)

    w_in_t = jnp.swapaxes(w_in[0], 0, 1)
    w_if_t = jnp.pad(w_in_t[ROW_IF:], ((0, LANES - 2 * M_HEADS), (0, 0)))
    gates, w_mix = _gate_proj(u_mix, w_in_t)

    per_block = lambda v: v.reshape(-1, RG_BLOCKS, RG_BLOCK).swapaxes(0, 1)
    y_rg, w2_gate, w2_up, w2_down = _rg_mixer(
        u_mix, w_mix, per_block(rg_conv_w[0]), per_block(rg_conv_b),
        bf(rg_w_a[0]), rg_b_a[0][:, None, :], bf(rg_w_x[0]), rg_b_x[0][:, None, :],
        per_block(rg_lambda), side=(ffn2_w_gate[0], ffn2_w_up[0], ffn2_w_down[0]))

    b_if = jnp.pad(jnp.concatenate([m_b_i, m_b_f], axis=1), ((0, 0), (0, LANES - 2 * M_HEADS)))
    y_m, wp_rg, wp_m, wp_out = _m_mixer(
        u_mix, w_mix, bf(w_if_t), m_conv_w[0], m_conv_b, bf(m_w_q[0]), bf(m_w_k[0]),
        bf(m_w_v[0]), b_if, m_norm, m_skip, side=(w_proj_rg[0], w_proj_m[0], w_out[0]))

    h = _merge(y_rg, y_m, gates, wp_rg, wp_m, wp_out, h)

    h = _ffn(h, ffn2_norm, w2_gate, w2_up, w2_down, final_norm[None, :], final_norm=True)
    return h[None]
```

```python
import functools

import jax
import jax.numpy as jnp
from jax import lax
from jax.experimental import pallas as pl
from jax.experimental.pallas import tpu as pltpu

D_MODEL = 2048
SEQ = 8192
D_FF = 5632
CONV_WIDTH = 4
RG_WIDTH = 2048
RG_BLOCKS = 8
RG_BLOCK = RG_WIDTH // RG_BLOCKS
RG_C = 8.0
M_HEADS = 8
M_HEAD_DIM = 256
M_WIDTH = M_HEADS * M_HEAD_DIM
M_CHUNK = 128
NORM_EPS = 1e-6

LANES = 128
SUBLANES = 8
V7X_VMEM_BYTES = 64 * 1024 * 1024
VMEM_LIMIT_BYTES = (V7X_VMEM_BYTES // 16) * 15
VMEM_LIMIT_WIDE_BYTES = (V7X_VMEM_BYTES // 64) * 63

ROW_RG = 0
ROW_M = 2 * RG_WIDTH
ROW_GATES = ROW_M + 2 * M_WIDTH
ROW_IF = ROW_GATES + 2 * D_MODEL

FFN_TM, FFN_TF = 1024, 256
FFN_TF_BF16 = 512
FFN_ROW_CHUNK = 128
GATE_TM, GATE_TN = 2048, 512
MIX_TM = 256
M_PROJ_PIECE = 512
MERGE_TM = 256

BF16 = jnp.bfloat16
F32 = jnp.float32
_NT_DIMS = (((1,), (1,)), ((), ()))


def _rms_norm(x, g):
    ms = jnp.mean(x * x, axis=-1, keepdims=True)
    return x * lax.rsqrt(ms + NORM_EPS) * g


def _softplus(z):
    return jnp.maximum(z, 0.0) + jnp.log1p(jnp.exp(-jnp.abs(z)))


def _compiler_params(semantics, vmem_limit=None):
    return pltpu.CompilerParams(dimension_semantics=semantics,
                                vmem_limit_bytes=vmem_limit or VMEM_LIMIT_BYTES)


def _resident(shape, col_block=0):
    index = (0,) * (len(shape) - 1) + (col_block,)
    return pl.BlockSpec(shape, lambda *_: index, pipeline_mode=pl.Buffered(1))


def _ffn_kernel(*refs, final_norm, tf):
    x_ref, g_ref, wg_hbm, wu_hbm, wd_hbm, eg_ref = refs[:6]
    if final_norm:
        o_ref, u_ref, wg_buf, wu_buf, wd_buf, sem = refs[6:]
    else:
        o_ref, un_ref, u_ref, wg_buf, wu_buf, wd_buf, sem = refs[6:]
    i = pl.program_id(0)
    nj = wg_hbm.shape[1] // tf
    n_tiles = pl.num_programs(0) * nj

    def tile_copies(j, slot):
        cols = pl.ds(pl.multiple_of(j * tf, tf), tf)
        return (pltpu.make_async_copy(wg_hbm.at[:, cols], wg_buf.at[slot], sem.at[0, slot]),
                pltpu.make_async_copy(wu_hbm.at[:, cols], wu_buf.at[slot], sem.at[1, slot]),
                pltpu.make_async_copy(wd_hbm.at[cols, :], wd_buf.at[slot], sem.at[2, slot]))

    def start_tile(j, slot):
        for k, copy in enumerate(tile_copies(j, slot)):
            copy.start(priority=k % 2)

    @pl.when(i == 0)
    def _():
        start_tile(0, 0)

    row_chunks = [pl.ds(r, FFN_ROW_CHUNK) for r in range(0, x_ref.shape[0], FFN_ROW_CHUNK)]
    for rows in row_chunks:
        u_ref[rows, :] = _rms_norm(x_ref[rows, :], g_ref[...]).astype(BF16)
    o_ref[...] = jnp.zeros_like(o_ref)

    def hidden_tile(j, carry):
        t = i * nj + j
        slot = lax.rem(t, 2)

        @pl.when(t + 1 < n_tiles)
        def _():
            start_tile(lax.rem(j + 1, nj), 1 - slot)

        for copy in tile_copies(j, slot):
            copy.wait()
        u = u_ref[...]
        gate = jnp.dot(u, wg_buf[slot].astype(BF16), preferred_element_type=F32)
        up = jnp.dot(u, wu_buf[slot].astype(BF16), preferred_element_type=F32)
        act = (gate * jax.nn.sigmoid(gate) * up).astype(BF16)
        o_ref[...] += jnp.dot(act, wd_buf[slot].astype(BF16), preferred_element_type=F32)
        return carry

    lax.fori_loop(0, nj, hidden_tile, 0)

    for rows in row_chunks:
        h = x_ref[rows, :] + 0.5 * o_ref[rows, :]
        if final_norm:
            o_ref[rows, :] = _rms_norm(h, eg_ref[...])
        else:
            o_ref[rows, :] = h
            un_ref[rows, :] = _rms_norm(h, eg_ref[...]).astype(BF16)


def _ffn(x, norm_g, wg, wu, wd, extra_g, final_norm):
    s, d = x.shape
    f = wg.shape[1]
    tm = FFN_TM
    bf16_weights = wg.dtype == BF16
    tf = FFN_TF_BF16 if bf16_weights else FFN_TF
    vmem_limit = VMEM_LIMIT_WIDE_BYTES if bf16_weights else VMEM_LIMIT_BYTES
    assert f % tf == 0 and s % tm == 0
    row_spec = pl.BlockSpec((tm, d), lambda i: (i, 0))
    gain_spec = pl.BlockSpec((1, d), lambda i: (0, 0))
    hbm_spec = pl.BlockSpec(memory_space=pl.ANY)
    out_specs, out_shape = row_spec, jax.ShapeDtypeStruct((s, d), F32)
    if not final_norm:
        out_specs = [row_spec, row_spec]
        out_shape = [out_shape, jax.ShapeDtypeStruct((s, d), BF16)]
    return pl.pallas_call(
        functools.partial(_ffn_kernel, final_norm=final_norm, tf=tf),
        grid=(s // tm,),
        in_specs=[row_spec, gain_spec, hbm_spec, hbm_spec, hbm_spec, gain_spec],
        out_specs=out_specs,
        out_shape=out_shape,
        scratch_shapes=[
            pltpu.VMEM((tm, d), BF16),
            pltpu.VMEM((2, d, tf), wg.dtype),
            pltpu.VMEM((2, d, tf), wu.dtype),
            pltpu.VMEM((2, tf, d), wd.dtype),
            pltpu.SemaphoreType.DMA((3, 2)),
        ],
        compiler_params=_compiler_params(("arbitrary",), vmem_limit),
        name="ffn_final" if final_norm else "ffn",
    )(x, norm_g, wg, wu, wd, extra_g)


def _gate_proj_kernel(u_ref, wt_ref, side_ref, o_ref, cast_ref):
    o_ref[...] = lax.dot_general(u_ref[...], wt_ref[...].astype(BF16), _NT_DIMS,
                                 preferred_element_type=F32)
    cast_ref[...] = side_ref[...].T.astype(BF16)


def _gate_proj(u, w_in_t):
    s, d = u.shape
    n = 2 * D_MODEL
    tm, tn = GATE_TM, GATE_TN
    row0 = ROW_GATES // tn
    nj = n // tn
    side_rows = ROW_GATES // ((s // tm) * nj)
    return pl.pallas_call(
        _gate_proj_kernel,
        grid=(s // tm, nj),
        in_specs=[
            pl.BlockSpec((tm, d), lambda i, j: (i, 0)),
            pl.BlockSpec((tn, d), lambda i, j: (row0 + j, 0)),
            pl.BlockSpec((side_rows, d), lambda i, j: (i * nj + j, 0)),
        ],
        out_specs=[
            pl.BlockSpec((tm, tn), lambda i, j: (i, j)),
            pl.BlockSpec((d, side_rows), lambda i, j: (0, i * nj + j)),
        ],
        out_shape=[jax.ShapeDtypeStruct((s, n), F32),
                   jax.ShapeDtypeStruct((d, ROW_GATES), BF16)],
        compiler_params=_compiler_params(("arbitrary", "arbitrary")),
        name="gate_proj",
    )(u, w_in_t, w_in_t)


def _causal_conv(x, xbuf_ref, cols, cw, cb, rows):
    xbuf_ref[pl.ds(SUBLANES, rows), cols] = x
    out = cb + cw[CONV_WIDTH - 1:CONV_WIDTH, :] * x
    for back in range(1, CONV_WIDTH):
        tap = CONV_WIDTH - 1 - back
        out = out + cw[tap:tap + 1, :] * xbuf_ref[pl.ds(SUBLANES - back, rows), cols]
    xbuf_ref[pl.ds(0, SUBLANES), cols] = xbuf_ref[pl.ds(rows, SUBLANES), cols]
    return out


def _skewed_steps(n_tiles):
    return (n_tiles + 1,
            lambda s: (jnp.minimum(s, n_tiles - 1), 0),
            lambda s: (jnp.maximum(s - 1, 0), 0))


def _rg_front(n, u, w_ref, dst_ref):
    cols_x = pl.ds(n * RG_BLOCK, RG_BLOCK)
    cols_g = pl.ds(RG_WIDTH + n * RG_BLOCK, RG_BLOCK)
    dst_ref[0, n] = jnp.dot(u, w_ref[:, cols_x], preferred_element_type=F32)
    dst_ref[1, n] = jnp.dot(u, w_ref[:, cols_g], preferred_element_type=F32)


def _rg_back(n, src_ref, obuf_ref, xbuf_ref, carry_ref, stage_ref, cw_ref, cb_ref, wa_ref, ba_ref,
             wx_ref, bx_ref, lam_ref):
    tm = obuf_ref.shape[1]
    group = (SUBLANES, RG_BLOCK)
    row = lax.broadcasted_iota(jnp.int32, group, 0)
    xc = _causal_conv(src_ref[0, n], xbuf_ref.at[n], slice(None), cw_ref[n], cb_ref[n], tm)
    xcb = xc.astype(BF16)
    stage_ref[0, n] = xc
    stage_ref[1, n] = jnp.dot(xcb, wa_ref[n], preferred_element_type=F32)
    stage_ref[2, n] = jnp.dot(xcb, wx_ref[n], preferred_element_type=F32)
    scale = jnp.broadcast_to(-RG_C * _softplus(-lam_ref[n]), group)
    ba = jnp.broadcast_to(ba_ref[n], group)
    bx = jnp.broadcast_to(bx_ref[n], group)

    carry = carry_ref[n]
    out_rows = 2 * SUBLANES
    hs = []
    for r0 in range(0, tm, SUBLANES):
        rows = pl.ds(r0, SUBLANES)
        xg = stage_ref[0, n, rows, :]
        r = jax.nn.sigmoid(stage_ref[1, n, rows, :] + ba)
        i = jax.nn.sigmoid(stage_ref[2, n, rows, :] + bx)
        log_a = r * scale
        a = jnp.exp(log_a)
        v = -jnp.tanh(log_a) * (a * a + 1.0)
        b = jnp.where(v > 0.0, v * lax.rsqrt(v), 0.0) * (i * xg)
        for k in (1, 2, 4):
            keep = row >= k
            a_prev = pltpu.roll(a, k, axis=0)
            b_prev = pltpu.roll(b, k, axis=0)
            b = jnp.where(keep, a * b_prev + b, b)
            a = jnp.where(keep, a * a_prev, a)
        h = b + a * carry
        carry = h[SUBLANES - 1:SUBLANES, :]
        hs.append(h * jax.nn.gelu(src_ref[1, n, rows, :]))
        if len(hs) * SUBLANES == out_rows:
            obuf_ref[n, pl.ds(r0 + SUBLANES - out_rows, out_rows), :] = (
                jnp.concatenate(hs, axis=0).astype(BF16))
            hs = []
    carry_ref[n] = carry


def _rg_mixer_kernel(u_ref, wt_ref, cw_ref, cb_ref, wa_ref, ba_ref, wx_ref, bx_ref, lam_ref,
                     side0_ref, side1_ref, side2_ref,
                     o_ref, cast0_ref, cast1_ref, cast2_ref,
                     p0_ref, p1_ref, xbuf_ref, carry_ref, obuf_ref, stage_ref):
    s = pl.program_id(0)

    for side_ref, cast_ref in ((side0_ref, cast0_ref), (side1_ref, cast1_ref),
                               (side2_ref, cast2_ref)):
        cast_ref[...] = side_ref[...].astype(BF16)

    @pl.when(s == 0)
    def _():
        p1_ref[...] = jnp.zeros_like(p1_ref)

    @pl.when(s <= 1)
    def _():
        xbuf_ref[:, pl.ds(0, SUBLANES), :] = jnp.zeros((RG_BLOCKS, SUBLANES, RG_BLOCK), F32)
        carry_ref[...] = jnp.zeros_like(carry_ref)

    def step(dst_ref, src_ref):
        u = u_ref[...]
        for n in range(RG_BLOCKS):
            _rg_front(n, u, wt_ref, dst_ref)
            _rg_back(n, src_ref, obuf_ref, xbuf_ref, carry_ref, stage_ref, cw_ref, cb_ref, wa_ref,
                     ba_ref, wx_ref, bx_ref, lam_ref)
            o_ref[:, pl.ds(n * RG_BLOCK, RG_BLOCK)] = obuf_ref[n]

    pl.when(s % 2 == 0)(lambda: step(p0_ref, p1_ref))
    pl.when(s % 2 == 1)(lambda: step(p1_ref, p0_ref))


def _rg_mixer(u, w_t, conv_w, conv_b, w_a, b_a, w_x, b_x, lam, side):
    s, d = u.shape
    tm = MIX_TM
    n_tiles = s // tm
    steps, cur_map, prev_map = _skewed_steps(n_tiles)
    blk = RG_BLOCK
    side_specs = [pl.BlockSpec((w.shape[0] // n_tiles, w.shape[1]), cur_map) for w in side]
    return pl.pallas_call(
        _rg_mixer_kernel,
        grid=(steps,),
        in_specs=[
            pl.BlockSpec((tm, d), cur_map),
            _resident((d, 2 * RG_WIDTH), ROW_RG // (2 * RG_WIDTH)),
            _resident((RG_BLOCKS, CONV_WIDTH, blk)),
            _resident((RG_BLOCKS, 1, blk)),
            _resident((RG_BLOCKS, blk, blk)),
            _resident((RG_BLOCKS, 1, blk)),
            _resident((RG_BLOCKS, blk, blk)),
            _resident((RG_BLOCKS, 1, blk)),
            _resident((RG_BLOCKS, 1, blk)),
        ] + side_specs,
        out_specs=[pl.BlockSpec((tm, RG_WIDTH), prev_map)] + side_specs,
        out_shape=([jax.ShapeDtypeStruct((s, RG_WIDTH), BF16)]
                   + [jax.ShapeDtypeStruct(w.shape, BF16) for w in side]),
        scratch_shapes=[
            pltpu.VMEM((2, RG_BLOCKS, tm, blk), F32),
            pltpu.VMEM((2, RG_BLOCKS, tm, blk), F32),
            pltpu.VMEM((RG_BLOCKS, tm + SUBLANES, blk), F32),
            pltpu.VMEM((RG_BLOCKS, 1, blk), F32),
            pltpu.VMEM((RG_BLOCKS, tm, blk), BF16),
            pltpu.VMEM((3, RG_BLOCKS, tm, blk), F32),
        ],
        compiler_params=_compiler_params(("arbitrary",)),
        name="rg_mixer",
    )(u, w_t, conv_w, conv_b, w_a, b_a, w_x, b_x, lam, *side)


def _mlstm_tile_qkv(ux, xbuf_ref, cw_ref, cb_ref, wq_ref, wk_ref, wv_ref):
    tm, D = ux.shape[0], M_HEAD_DIM
    conv = _causal_conv(ux, xbuf_ref, slice(None), cw_ref[...], cb_ref[...], tm)
    xc = conv * jax.nn.sigmoid(conv)
    heads = range(M_HEADS)
    sls = [slice(h * D, (h + 1) * D) for h in heads]
    xcb = [xc[:, sl].astype(BF16) for sl in sls]
    q = [jnp.dot(xcb[h], wq_ref[h], preferred_element_type=F32) for h in heads]
    kk = [jnp.dot(xcb[h], wk_ref[h], preferred_element_type=F32) * (M_HEAD_DIM ** -0.5)
          for h in heads]
    v = [jnp.dot(ux[:, sls[h]].astype(BF16), wv_ref[h], preferred_element_type=F32)
         for h in heads]
    return xc, q, kk, v


def _mlstm_chunk(xc, q, kk, v, mo, gif, o_ref, rows, c_ref, n_ref, m_ref, bif_ref, ng_ref,
                 skip_ref):
    L, D = M_CHUNK, M_HEAD_DIM

    gt = (gif + bif_ref[...]).T
    log_i = gt[0:M_HEADS, :]
    log_f = -_softplus(-gt[M_HEADS:2 * M_HEADS, :])
    lane = lax.broadcasted_iota(jnp.int32, (M_HEADS, L), 1)
    b = log_f
    k = 1
    while k < L:
        b = b + jnp.where(lane >= k, pltpu.roll(b, k, axis=1), 0.0)
        k *= 2
    g = b[:, L - 1:L]
    r = log_i - b
    a = g + r
    m_loc = jnp.max(a, axis=1, keepdims=True)
    w = jnp.exp(a - m_loc)
    m_prev = m_ref[:, 0:1]
    m_new = jnp.maximum(g + m_prev, m_loc)
    s_old = jnp.exp(g + m_prev - m_new)
    s_loc = jnp.exp(m_loc - m_new)
    m_ref[...] = jnp.broadcast_to(m_new, m_ref.shape)

    rmax = r
    k = 1
    while k < L:
        rmax = jnp.maximum(rmax, jnp.where(lane >= k, pltpu.roll(rmax, k, axis=1), -jnp.inf))
        k *= 2
    cm = jnp.maximum(m_prev, rmax)
    iw = jnp.exp(m_prev - cm)
    em = jnp.exp(-(b + cm))

    cols = jnp.concatenate([cm, w, iw, em, jnp.zeros((LANES - 4 * M_HEADS, L), F32)], axis=0).T

    tri = (lax.broadcasted_iota(jnp.int32, (L, L), 0)
           >= lax.broadcasted_iota(jnp.int32, (L, L), 1))

    heads = range(M_HEADS)
    sls = [slice(h * D, (h + 1) * D) for h in heads]

    qb = [x.astype(BF16) for x in q]
    kb = [x.astype(BF16) for x in kk]
    qk = [lax.dot_general(qb[h], kb[h], _NT_DIMS, preferred_element_type=F32) for h in heads]

    col = lambda which, h: cols[:, which * M_HEADS + h:which * M_HEADS + h + 1]
    cm_col = [col(0, h) for h in heads]
    w_col = [col(1, h) for h in heads]
    inter_w = [col(2, h) for h in heads]
    em_col = [col(3, h) for h in heads]
    c_prev = [c_ref[h] for h in heads]
    n_prev = [n_ref[h] for h in heads]
    qc = [jnp.dot(qb[h], c_prev[h].astype(BF16), preferred_element_type=F32) for h in heads]
    c_loc = [jnp.dot(kk[h].T.astype(BF16), (w_col[h] * v[h]).astype(BF16),
                     preferred_element_type=F32) for h in heads]

    scores = [qk[h] * jnp.exp(jnp.where(tri, r[h:h + 1, :] - cm_col[h], -jnp.inf))
              for h in heads]
    sv = [jnp.dot(scores[h].astype(BF16), v[h].astype(BF16), preferred_element_type=F32)
          for h in heads]

    for h in heads:
        s_old_h = s_old[h:h + 1, :]
        s_loc_h = s_loc[h:h + 1, :]
        c_ref[h] = s_old_h * c_prev[h] + s_loc_h * c_loc[h]
        n_loc = jnp.sum(w_col[h] * kk[h], axis=0, keepdims=True)
        n_ref[h] = s_old_h * n_prev[h] + s_loc_h * n_loc

    for h in heads:
        sl = sls[h]
        num = sv[h] + inter_w[h] * qc[h]
        den = (jnp.sum(scores[h], axis=1, keepdims=True)
               + inter_w[h] * jnp.sum(q[h] * n_prev[h], axis=1, keepdims=True))
        hh = num * (1.0 / jnp.maximum(jnp.abs(den), em_col[h]))
        mu = jnp.mean(hh, axis=1, keepdims=True)
        cen = hh - mu
        var = jnp.mean(cen * cen, axis=1, keepdims=True)
        y = cen * lax.rsqrt(var + NORM_EPS) * ng_ref[:, sl] + skip_ref[:, sl] * xc[:, sl]
        o_ref[rows, sl] = (jax.nn.sigmoid(mo[:, sl]) * y).astype(o_ref.dtype)


def _m_mixer_kernel(u_ref, wt_ref, wift_ref, cw_ref, cb_ref, wq_ref, wk_ref, wv_ref, bif_ref,
                    ng_ref, skip_ref, side0_ref, side1_ref, side2_ref,
                    o_ref, cast0_ref, cast1_ref, cast2_ref,
                    p0_ref, p1_ref, g0_ref, g1_ref, xbuf_ref, c_ref, n_ref, m_ref):
    s = pl.program_id(0)

    for side_ref, cast_ref in ((side0_ref, cast0_ref), (side1_ref, cast1_ref),
                               (side2_ref, cast2_ref)):
        cast_ref[...] = side_ref[...].astype(BF16)

    @pl.when(s == 0)
    def _():
        p1_ref[...] = jnp.zeros_like(p1_ref)
        g1_ref[...] = jnp.zeros_like(g1_ref)

    @pl.when(s <= 1)
    def _():
        xbuf_ref[pl.ds(0, SUBLANES), :] = jnp.zeros((SUBLANES, M_WIDTH), F32)
        c_ref[...] = jnp.zeros_like(c_ref)
        n_ref[...] = jnp.zeros_like(n_ref)
        m_ref[...] = jnp.zeros_like(m_ref)

    def step(dst_ref, gdst_ref, src_ref, gsrc_ref):
        u = u_ref[...]
        for c0 in range(0, 2 * M_WIDTH, M_PROJ_PIECE):
            cols = pl.ds(c0, M_PROJ_PIECE)
            dst_ref[:, cols] = jnp.dot(u, wt_ref[:, cols], preferred_element_type=F32)
        gdst_ref[...] = lax.dot_general(u, wift_ref[...], _NT_DIMS, preferred_element_type=F32)

        xc, q, kk, v = _mlstm_tile_qkv(src_ref[:, pl.ds(0, M_WIDTH)], xbuf_ref, cw_ref, cb_ref,
                                       wq_ref, wk_ref, wv_ref)
        for r0 in range(0, o_ref.shape[0], M_CHUNK):
            rows = pl.ds(r0, M_CHUNK)
            part = lambda xs: [x[r0:r0 + M_CHUNK, :] for x in xs]
            _mlstm_chunk(xc[r0:r0 + M_CHUNK, :], part(q), part(kk), part(v),
                         src_ref[rows, pl.ds(M_WIDTH, M_WIDTH)], gsrc_ref[rows, :], o_ref, rows,
                         c_ref, n_ref, m_ref, bif_ref, ng_ref, skip_ref)

    pl.when(s % 2 == 0)(lambda: step(p0_ref, g0_ref, p1_ref, g1_ref))
    pl.when(s % 2 == 1)(lambda: step(p1_ref, g1_ref, p0_ref, g0_ref))


def _m_mixer(u, w_t, w_if_t, conv_w, conv_b, w_q, w_k, w_v, b_if, norm_g, skip, side):
    s, d = u.shape
    tm = MIX_TM
    n_tiles = s // tm
    steps, cur_map, prev_map = _skewed_steps(n_tiles)
    head_w = (M_HEADS, M_HEAD_DIM, M_HEAD_DIM)
    side_specs = [pl.BlockSpec((w.shape[0] // n_tiles, w.shape[1]), cur_map) for w in side]
    return pl.pallas_call(
        _m_mixer_kernel,
        grid=(steps,),
        in_specs=[
            pl.BlockSpec((tm, d), cur_map),
            _resident((d, 2 * M_WIDTH), ROW_M // (2 * M_WIDTH)),
            _resident((LANES, d)),
            _resident((CONV_WIDTH, M_WIDTH)),
            _resident((1, M_WIDTH)),
            _resident(head_w),
            _resident(head_w),
            _resident(head_w),
            _resident((1, LANES)),
            _resident((1, M_WIDTH)),
            _resident((1, M_WIDTH)),
        ] + side_specs,
        out_specs=[pl.BlockSpec((tm, M_WIDTH), prev_map)] + side_specs,
        out_shape=([jax.ShapeDtypeStruct((s, M_WIDTH), BF16)]
                   + [jax.ShapeDtypeStruct(w.shape, BF16) for w in side]),
        scratch_shapes=[
            pltpu.VMEM((tm, 2 * M_WIDTH), F32),
            pltpu.VMEM((tm, 2 * M_WIDTH), F32),
            pltpu.VMEM((tm, LANES), F32),
            pltpu.VMEM((tm, LANES), F32),
            pltpu.VMEM((tm + SUBLANES, M_WIDTH), F32),
            pltpu.VMEM(head_w, F32),
            pltpu.VMEM((M_HEADS, 1, M_HEAD_DIM), F32),
            pltpu.VMEM((M_HEADS, LANES), F32),
        ],
        compiler_params=_compiler_params(("arbitrary",)),
        name="m_mixer",
    )(u, w_t, w_if_t, conv_w, conv_b, w_q, w_k, w_v, b_if, norm_g, skip, *side)


def _merge_kernel(yrg_ref, ym_ref, grg_ref, gm_ref, wrg_ref, wm_ref, wout_ref, h_ref, o_ref):
    merged = (jax.nn.sigmoid(grg_ref[...])
              * jnp.dot(yrg_ref[...], wrg_ref[...], preferred_element_type=F32)
              + jax.nn.sigmoid(gm_ref[...])
              * jnp.dot(ym_ref[...], wm_ref[...], preferred_element_type=F32))
    o_ref[...] = h_ref[...] + jnp.dot(merged.astype(BF16), wout_ref[...],
                                      preferred_element_type=F32)


def _merge(y_rg, y_m, gates, w_rg, w_m, w_out, h1):
    s, d = h1.shape
    tm = MERGE_TM
    return pl.pallas_call(
        _merge_kernel,
        grid=(s // tm,),
        in_specs=[
            pl.BlockSpec((tm, RG_WIDTH), lambda i: (i, 0)),
            pl.BlockSpec((tm, M_WIDTH), lambda i: (i, 0)),
            pl.BlockSpec((tm, d), lambda i: (i, 0)),
            pl.BlockSpec((tm, d), lambda i: (i, 1)),
            _resident((RG_WIDTH, d)),
            _resident((M_WIDTH, d)),
            _resident((d, d)),
            pl.BlockSpec((tm, d), lambda i: (i, 0)),
        ],
        out_specs=pl.BlockSpec((tm, d), lambda i: (i, 0)),
        out_shape=jax.ShapeDtypeStruct((s, d), F32),
        compiler_params=_compiler_params(("parallel",)),
        name="merge",
    )(y_rg, y_m, gates, gates, w_rg, w_m, w_out, h1)


def kernel(x, ffn1_norm, ffn1_w_gate, ffn1_w_up, ffn1_w_down, mix_norm, w_in, rg_conv_w, rg_conv_b, rg_w_a, rg_b_a, rg_w_x, rg_b_x, rg_lambda, m_conv_w, m_conv_b, m_w_q, m_w_k, m_w_v, m_b_i, m_b_f, m_norm, m_skip, w_proj_rg, w_proj_m, w_out, ffn2_norm, ffn2_w_gate, ffn2_w_up, ffn2_w_down, final_norm):
    bsz, s, d = x.shape
    assert (bsz, s, d) == (1, SEQ, D_MODEL) and ffn1_norm.shape[0] == 1
    bf = lambda w: w.astype(BF16)
    h = x[0]

    h, u_mix = _ffn(h, ffn1_norm, ffn1_w_gate[0], ffn1_w_up[0], ffn1_w_down[0], mix_norm,
                    final_norm=False)

    w_in_t = jnp.swapaxes(w_in[0], 0, 1)
    w_if_t = jnp.pad(w_in_t[ROW_IF:], ((0, LANES - 2 * M_HEADS), (0, 0)))
    gates, w_mix = _gate_proj(u_mix, w_in_t)

    per_block = lambda v: v.reshape(-1, RG_BLOCKS, RG_BLOCK).swapaxes(0, 1)
    y_rg, w2_gate, w2_up, w2_down = _rg_mixer(
        u_mix, w_mix, per_block(rg_conv_w[0]), per_block(rg_conv_b),
        bf(rg_w_a[0]), rg_b_a[0][:, None, :], bf(rg_w_x[0]), rg_b_x[0][:, None, :],
        per_block(rg_lambda), side=(ffn2_w_gate[0], ffn2_w_up[0], ffn2_w_down[0]))

    b_if = jnp.pad(jnp.concatenate([m_b_i, m_b_f], axis=1), ((0, 0), (0, LANES - 2 * M_HEADS)))
    y_m, wp_rg, wp_m, wp_out = _m_mixer(
        u_mix, w_mix, bf(w_if_t), m_conv_w[0], m_conv_b, bf(m_w_q[0]), bf(m_w_k[0]),
        bf(m_w_v[0]), b_if, m_norm, m_skip, side=(w_proj_rg[0], w_proj_m[0], w_out[0]))

    h = _merge(y_rg, y_m, gates, wp_rg, wp_m, wp_out, h)

    h = _ffn(h, ffn2_norm, w2_gate, w2_up, w2_down, final_norm[None, :], final_norm=True)
    return h[None]
```
